```python
import math
import jax, jax.numpy as jnp
from jax import lax
import numpy as np

D_MODEL = 1024
BATCH = 4
SEQ = 8192
DEPTH = 4

CHUNK = 64
Q_BLOCK = 128
N_MIXERS = 3
N_ATTN = (DEPTH + 2) // 3
N_RWKV = (DEPTH + 1) // 3
N_POOL = DEPTH // 3

DA_HEADS = 8
DA_HEAD_DIM = D_MODEL // DA_HEADS // 2
ROPE_THETA = 10000.0
QK_EPS = 1e-6
SUBLN_EPS = 1e-5

RW_HEAD_DIM = 64
RW_HEADS = D_MODEL // RW_HEAD_DIM
RW_DECAY_LORA = max(32, int(round(1.8 * D_MODEL ** 0.5 / 32)) * 32)
RW_AAA_LORA = max(32, int(round(1.8 * D_MODEL ** 0.5 / 32)) * 32)
RW_GATE_LORA = max(32, int(round(0.6 * D_MODEL ** 0.8 / 32)) * 32)
RW_LNX_EPS = 64e-5

POOL_WINDOWS = (2, 4, 8, 16)
POOL_GROUP = D_MODEL // len(POOL_WINDOWS)

FFN_HIDDEN = -(-8 * D_MODEL // (3 * 256)) * 256
NORM_EPS = 1e-6

kernel_name = "hybrid_diffattn_rwkv7_pool_trunk"


def rms_norm(x, g, eps=NORM_EPS):
    xf = x.astype(jnp.float32)
    y = xf * lax.rsqrt(jnp.mean(xf * xf, axis=-1, keepdims=True) + eps)
    return (y * g.astype(jnp.float32)).astype(x.dtype)


def rope_tables(T, dim):
    inv = 1.0 / (ROPE_THETA ** (jnp.arange(0, dim, 2, dtype=jnp.float32) / dim))
    ang = jnp.arange(T, dtype=jnp.float32)[:, None] * inv[None, :]
    return jnp.cos(ang), jnp.sin(ang)


def apply_rope(x, cos, sin):
    xf = x.astype(jnp.float32)
    half = xf.shape[-1] // 2
    x1, x2 = xf[..., :half], xf[..., half:]
    out = jnp.concatenate([x1 * cos - x2 * sin, x2 * cos + x1 * sin], axis=-1)
    return out.astype(x.dtype)


def diff_attention(h, w_qkv, w_o, q_gain, k_gain, lam_vec, subln_g, lambda_init):
    B, T, C = h.shape
    H, d = DA_HEADS, DA_HEAD_DIM
    qkv = h @ w_qkv
    q, k, v = jnp.split(qkv, 3, axis=-1)
    q = q.reshape(B, T, 2 * H, d).transpose(0, 2, 1, 3)
    k = k.reshape(B, T, 2 * H, d).transpose(0, 2, 1, 3)
    v = v.reshape(B, T, H, 2 * d).transpose(0, 2, 1, 3)
    q = rms_norm(q, q_gain, QK_EPS)
    k = rms_norm(k, k_gain, QK_EPS)
    cos, sin = rope_tables(T, d)
    q = apply_rope(q, cos, sin)
    k = apply_rope(k, cos, sin)

    lv = lam_vec.astype(jnp.float32)
    lam = jnp.exp(jnp.sum(lv[0] * lv[1])) - jnp.exp(jnp.sum(lv[2] * lv[3])) + lambda_init

    nb = T // Q_BLOCK
    q_blocks = q.reshape(B, 2 * H, nb, Q_BLOCK, d).transpose(2, 0, 1, 3, 4)
    k_chunk = jnp.arange(T) // CHUNK
    scale = d ** -0.5
    neg = jnp.float32(-1e30)

    def block(args):
        qb, bi = args
        s = jnp.einsum('bhqd,bhkd->bhqk', qb, k).astype(jnp.float32) * scale
        q_chunk = (bi * Q_BLOCK + jnp.arange(Q_BLOCK)) // CHUNK
        mask = k_chunk[None, :] <= q_chunk[:, None]
        s = jnp.where(mask[None, None], s, neg)
        p = jax.nn.softmax(s, axis=-1).reshape(B, H, 2, Q_BLOCK, T)
        a = p[:, :, 0] - lam * p[:, :, 1]
        return jnp.einsum('bhqk,bhkd->bhqd', a.astype(v.dtype), v)

    o = lax.map(block, (q_blocks, jnp.arange(nb)))
    o = o.transpose(1, 2, 0, 3, 4).reshape(B, H, T, 2 * d)
    o = rms_norm(o, subln_g, SUBLN_EPS) * jnp.asarray(1.0 - lambda_init, o.dtype)
    o = o.transpose(0, 2, 1, 3).reshape(B, T, C)
    return o @ w_o


def rwkv7_time_mix(h, mix, w_rkv, w_o, w0, w1, w2, a0, a1, a2, g1, g2,
                   k_k, k_a, r_k, lnx_g, lnx_b):
    B, T, C = h.shape
    H, N = RW_HEADS, RW_HEAD_DIM
    f32 = jnp.float32
    h_prev = jnp.pad(h, ((0, 0), (1, 0), (0, 0)))[:, :-1]
    dx = h_prev - h
    xr = h + dx * mix[0]
    xw = h + dx * mix[1]
    xk = h + dx * mix[2]
    xv = h + dx * mix[3]
    xa = h + dx * mix[4]
    xg = h + dx * mix[5]

    r = (xr @ w_rkv[:, :C]).astype(f32)
    k = (xk @ w_rkv[:, C:2 * C]).astype(f32)
    v = (xv @ w_rkv[:, 2 * C:]).astype(f32)
    w_log = -jax.nn.softplus(-(w0.astype(f32) + (jnp.tanh(xw @ w1) @ w2).astype(f32))) - 0.5
    decay = jnp.exp(-jnp.exp(w_log))
    a = jax.nn.sigmoid(a0.astype(f32) + ((xa @ a1) @ a2).astype(f32))
    g = (jax.nn.sigmoid(xg @ g1) @ g2).astype(f32)

    kk = (k * k_k.astype(f32)).reshape(B, T, H, N)
    kk = kk * lax.rsqrt(jnp.maximum(jnp.sum(kk * kk, axis=-1, keepdims=True), 1e-24))
    k = k * (1.0 + (a - 1.0) * k_a.astype(f32))

    def heads_t(z):
        return z.reshape(B, T, H, N).transpose(1, 0, 2, 3)

    a_h = a.reshape(B, T, H, N)
    seq_in = (heads_t(r), heads_t(decay), heads_t(k), heads_t(v),
              (-kk).transpose(1, 0, 2, 3), (kk * a_h).transpose(1, 0, 2, 3))

    def step(S, inp):
        r_t, w_t, k_t, v_t, va_t, vb_t = inp
        sa = jnp.einsum('bhij,bhj->bhi', S, va_t)
        S = S * w_t[:, :, None, :] + sa[..., None] * vb_t[:, :, None, :] + v_t[..., None] * k_t[:, :, None, :]
        y = jnp.einsum('bhij,bhj->bhi', S, r_t)
        return S, y

    S0 = jnp.zeros((B, H, N, N), f32)
    _, y = lax.scan(step, S0, seq_in)
    y = y.transpose(1, 0, 2, 3)

    mu = jnp.mean(y, axis=-1, keepdims=True)
    var = jnp.mean(jnp.square(y - mu), axis=-1, keepdims=True)
    yn = ((y - mu) * lax.rsqrt(var + RW_LNX_EPS)).reshape(B, T, C)
    yn = yn * lnx_g.astype(f32) + lnx_b.astype(f32)

    bonus = jnp.sum(r.reshape(B, T, H, N) * k.reshape(B, T, H, N) * r_k.astype(f32),
                    axis=-1, keepdims=True) * v.reshape(B, T, H, N)
    out = (yn + bonus.reshape(B, T, C)) * g
    return out.astype(h.dtype) @ w_o


def multiscale_pool(h, w_groups, scale):
    B, T, C = h.shape
    hg = h.astype(jnp.float32).reshape(B, T, len(POOL_WINDOWS), POOL_GROUP)
    outs = []
    for gi, win in enumerate(POOL_WINDOWS):
        xg = hg[:, :, gi]
        P = jnp.cumsum(jnp.pad(xg, ((0, 0), (win, 0), (0, 0))), axis=1)
        s = P[:, win:] - P[:, :T]
        cnt = jnp.minimum(jnp.arange(1, T + 1), win).astype(jnp.float32)
        outs.append(s / cnt[None, :, None] - xg)
    dpool = jnp.stack(outs, axis=2).astype(h.dtype)
    y = jnp.einsum('btgi,gio->btgo', dpool, w_groups)
    return y.reshape(B, T, C) * scale


def swiglu(h, w_in, w_out):
    gu = h @ w_in
    gate, up = jnp.split(gu, 2, axis=-1)
    return (jax.nn.silu(gate) * up) @ w_out


def setup_inputs(seed: int = 0) -> dict:
    key = jax.random.key(seed)
    ks = iter(jax.random.split(key, 40))
    C, F = D_MODEL, FFN_HIDDEN
    d = DA_HEAD_DIM
    f32 = jnp.float32

    def nrm(shape, s):
        return jax.random.normal(next(ks), shape, f32) * s

    x = nrm((BATCH, SEQ, C), 1.0)
    mix_norm = 1.0 + nrm((DEPTH, C), 0.05)
    ffn_norm = 1.0 + nrm((DEPTH, C), 0.05)
    ffn_w_in = nrm((DEPTH, C, 2 * F), C ** -0.5)
    ffn_w_out = nrm((DEPTH, F, C), F ** -0.5)

    da_wqkv = nrm((N_ATTN, C, 3 * C), C ** -0.5)
    da_wo = nrm((N_ATTN, C, C), C ** -0.5)
    da_q_gain = 1.0 + nrm((N_ATTN, d), 0.05)
    da_k_gain = 1.0 + nrm((N_ATTN, d), 0.05)
    da_lambda = nrm((N_ATTN, 4, d), 0.1)
    da_subln = 1.0 + nrm((N_ATTN, 2 * d), 0.05)

    rw_mix = jax.random.uniform(next(ks), (N_RWKV, 6, C), f32)
    rw_wrkv = nrm((N_RWKV, C, 3 * C), C ** -0.5)
    rw_wo = nrm((N_RWKV, C, C), C ** -0.5)
    ramp = -7.0 + 5.0 * (jnp.arange(C, dtype=f32) / (C - 1)) ** 0.85 + 0.5
    rw_w0 = ramp[None, :] + nrm((N_RWKV, C), 0.1)
    rw_w1 = nrm((N_RWKV, C, RW_DECAY_LORA), C ** -0.5)
    rw_w2 = nrm((N_RWKV, RW_DECAY_LORA, C), 0.1 * RW_DECAY_LORA ** -0.5)
    rw_a0 = nrm((N_RWKV, C), 0.1)
    rw_a1 = nrm((N_RWKV, C, RW_AAA_LORA), C ** -0.5)
    rw_a2 = nrm((N_RWKV, RW_AAA_LORA, C), RW_AAA_LORA ** -0.5)
    rw_g1 = nrm((N_RWKV, C, RW_GATE_LORA), C ** -0.5)
    rw_g2 = nrm((N_RWKV, RW_GATE_LORA, C), RW_GATE_LORA ** -0.5)
    rw_kk = 0.85 + nrm((N_RWKV, C), 0.05)
    rw_ka = 1.0 + nrm((N_RWKV, C), 0.05)
    rw_rk = nrm((N_RWKV, RW_HEADS, RW_HEAD_DIM), 0.1)
    rw_lnx_g = 1.0 + nrm((N_RWKV, C), 0.05)
    rw_lnx_b = nrm((N_RWKV, C), 0.01)

    pool_w = nrm((N_POOL, len(POOL_WINDOWS), POOL_GROUP, POOL_GROUP), POOL_GROUP ** -0.5)
    pool_scale = 1.0 + nrm((N_POOL, C), 0.1)

    return {
        "x": x, "mix_norm": mix_norm, "ffn_norm": ffn_norm,
        "ffn_w_in": ffn_w_in, "ffn_w_out": ffn_w_out,
        "da_wqkv": da_wqkv, "da_wo": da_wo, "da_q_gain": da_q_gain,
        "da_k_gain": da_k_gain, "da_lambda": da_lambda, "da_subln": da_subln,
        "rw_mix": rw_mix, "rw_wrkv": rw_wrkv, "rw_wo": rw_wo,
        "rw_w0": rw_w0, "rw_w1": rw_w1, "rw_w2": rw_w2,
        "rw_a0": rw_a0, "rw_a1": rw_a1, "rw_a2": rw_a2,
        "rw_g1": rw_g1, "rw_g2": rw_g2, "rw_kk": rw_kk, "rw_ka": rw_ka,
        "rw_rk": rw_rk, "rw_lnx_g": rw_lnx_g, "rw_lnx_b": rw_lnx_b,
        "pool_w": pool_w, "pool_scale": pool_scale,
    }


def reference(x, mix_norm, ffn_norm, ffn_w_in, ffn_w_out,
              da_wqkv, da_wo, da_q_gain, da_k_gain, da_lambda, da_subln,
              rw_mix, rw_wrkv, rw_wo, rw_w0, rw_w1, rw_w2, rw_a0, rw_a1, rw_a2,
              rw_g1, rw_g2, rw_kk, rw_ka, rw_rk, rw_lnx_g, rw_lnx_b,
              pool_w, pool_scale):
    ia = ir = ip = 0
    for layer in range(DEPTH):
        kind = layer % N_MIXERS
        h = rms_norm(x, mix_norm[layer])
        if kind == 0:
            lambda_init = 0.8 - 0.6 * math.exp(-0.3 * layer)
            y = diff_attention(h, da_wqkv[ia], da_wo[ia], da_q_gain[ia], da_k_gain[ia],
                               da_lambda[ia], da_subln[ia], lambda_init)
            ia += 1
        elif kind == 1:
            y = rwkv7_time_mix(h, rw_mix[ir], rw_wrkv[ir], rw_wo[ir], rw_w0[ir], rw_w1[ir], rw_w2[ir],
                               rw_a0[ir], rw_a1[ir], rw_a2[ir], rw_g1[ir], rw_g2[ir],
                               rw_kk[ir], rw_ka[ir], rw_rk[ir], rw_lnx_g[ir], rw_lnx_b[ir])
            ir += 1
        else:
            y = multiscale_pool(h, pool_w[ip], pool_scale[ip])
            ip += 1
        x = x + y
        x = x + swiglu(rms_norm(x, ffn_norm[layer]), ffn_w_in[layer], ffn_w_out[layer])
    return x
```

```python
import functools
import math

import jax
import jax.numpy as jnp
from jax import lax
from jax.experimental import pallas as pl
from jax.experimental.pallas import tpu as pltpu

F32 = jnp.float32
BF16 = jnp.bfloat16

LANES = 128
SUBLANES = 8
VMEM_LIMIT_BYTES = 48 * 1024 * 1024

CHUNK = 64
HEAD = 64
N_MIXERS = 3
NORM_EPS = 1e-6
QK_EPS = 1e-6
SUBLN_EPS = 1e-5
ROPE_THETA = 10000.0
RW_LNX_EPS = 64e-5
POOL_WINDOWS = (2, 4, 8, 16)
POOL_HALO = 16
NEG = -1e30


def _rms(z, gain, eps):
    return z * lax.rsqrt(jnp.mean(z * z, axis=-1, keepdims=True) + eps) * gain


def _bdot(a, b):
    return jnp.dot(a, b, preferred_element_type=F32)


def _dot_nt(a, b):
    return lax.dot_general(a, b, (((1,), (1,)), ((), ())), preferred_element_type=F32)


def _dot_tn(a, b):
    return lax.dot_general(a, b, (((0,), (0,)), ((), ())), preferred_element_type=F32)


def _half_sums(z, first):
    lo = jnp.sum(jnp.where(first, z, 0.0), axis=-1, keepdims=True)
    hi = jnp.sum(jnp.where(first, 0.0, z), axis=-1, keepdims=True)
    return jnp.where(first, lo, hi)


def _params(*sem):
    return pltpu.CompilerParams(dimension_semantics=sem, vmem_limit_bytes=VMEM_LIMIT_BYTES)


def _resident(shape):
    return pl.BlockSpec(shape, lambda *_: (0,) * len(shape), pipeline_mode=pl.Buffered(1))


def _mix_ffn_kernel(*refs, hidden, tf, has_scale):
    if has_scale:
        x_ref, y_ref, wm_ref, sc_ref, g_ref, win_ref, wout_ref, o_ref = refs
    else:
        x_ref, y_ref, wm_ref, g_ref, win_ref, wout_ref, o_ref = refs
    y = _bdot(y_ref[...], wm_ref[...])
    if has_scale:
        y = y * sc_ref[...]
    x1 = x_ref[...] + y
    hn = _rms(x1, g_ref[...], NORM_EPS).astype(BF16)
    acc = x1
    for f in range(hidden // tf):
        gate = _bdot(hn, win_ref[:, f * tf:(f + 1) * tf])
        up = _bdot(hn, win_ref[:, hidden + f * tf:hidden + (f + 1) * tf])
        act = (gate * jax.nn.sigmoid(gate) * up).astype(BF16)
        acc = acc + _bdot(act, wout_ref[f * tf:(f + 1) * tf, :])
    o_ref[...] = acc


def _mix_ffn(x, y, w_mix, scale, g, w_in, w_out, *, tm=512, tf=256):
    M, C = x.shape
    hidden = w_out.shape[0]
    has_scale = scale is not None
    row = lambda w: pl.BlockSpec((tm, w), lambda i: (i, 0))
    in_specs = [row(C), row(C), _resident((C, C))]
    args = [x, y, w_mix]
    if has_scale:
        in_specs.append(_resident((1, C)))
        args.append(scale.reshape(1, C))
    in_specs += [_resident((1, C)), _resident((C, 2 * hidden)), _resident((hidden, C))]
    args += [g.reshape(1, C), w_in, w_out]
    return pl.pallas_call(
        functools.partial(_mix_ffn_kernel, hidden=hidden, tf=tf, has_scale=has_scale),
        grid=(M // tm,),
        in_specs=in_specs,
        out_specs=row(C),
        out_shape=jax.ShapeDtypeStruct((M, C), F32),
        compiler_params=_params("parallel"),
        name="mix_ffn",
    )(*args)


def _qkv_kernel(x_ref, g_ref, w_ref, qg_ref, kg_ref, cos_ref, sin_ref, q_ref, k_ref, v_ref):
    C = x_ref.shape[1]
    hn = _rms(x_ref[...], g_ref[...], NORM_EPS).astype(BF16)
    qkv = _bdot(hn, w_ref[...])
    cos = cos_ref[...]
    sin = sin_ref[...]
    lane = lax.broadcasted_iota(jnp.int32, cos.shape, 1)
    first = lane < HEAD
    low_half = (lane & (HEAD - 1)) < HEAD // 2

    def norm_rope(z, gain):
        ms = _half_sums(z * z, first) * (1.0 / HEAD)
        zn = z * lax.rsqrt(ms + QK_EPS) * gain
        partner = jnp.where(low_half, pltpu.roll(zn, LANES - HEAD // 2, 1), pltpu.roll(zn, HEAD // 2, 1))
        return zn * cos + partner * sin

    for s in range(C // LANES):
        sl = slice(s * LANES, (s + 1) * LANES)
        q_ref[:, sl] = norm_rope(qkv[:, sl], qg_ref[...]).astype(BF16)
        k_ref[:, sl] = norm_rope(qkv[:, C + s * LANES:C + (s + 1) * LANES], kg_ref[...]).astype(BF16)
    v_ref[...] = qkv[:, 2 * C:].astype(BF16)


def _attn_kernel(q_ref, k_ref, v_ref, lam_ref, sg_ref, o_ref, *, tq, lambda_init):
    i = pl.program_id(2)
    q = q_ref[0]
    lane = lax.broadcasted_iota(jnp.int32, q.shape, 1)
    zero = jnp.zeros_like(q)
    qs = jnp.concatenate([jnp.where(lane < HEAD, q, zero), jnp.where(lane < HEAD, zero, q)], axis=0)

    def tile(j, carry, diagonal):
        m, l, acc = carry
        start = pl.multiple_of(j * tq, tq)
        s = _dot_nt(qs, k_ref[0, pl.ds(start, tq), :])
        if diagonal:
            row = lax.broadcasted_iota(jnp.int32, s.shape, 0)
            col = lax.broadcasted_iota(jnp.int32, s.shape, 1)
            row = jnp.where(row >= tq, row - tq, row)
            s = jnp.where((col // CHUNK) <= (row // CHUNK), s, NEG)
        m_new = jnp.maximum(m, jnp.max(s, axis=-1, keepdims=True))
        alpha = jnp.exp(m - m_new)
        p = jnp.exp(s - m_new)
        l = alpha * l + jnp.sum(p, axis=-1, keepdims=True)
        acc = alpha * acc + _bdot(p.astype(BF16), v_ref[0, pl.ds(start, tq), :])
        return m_new, l, acc

    carry = (jnp.full((2 * tq, 1), NEG, F32), jnp.zeros((2 * tq, 1), F32), jnp.zeros((2 * tq, LANES), F32))
    carry = lax.fori_loop(0, i, lambda j, c: tile(j, c, False), carry)
    _, l, acc = tile(i, carry, True)
    o = acc / l
    lv = lam_ref[...]
    lam = (jnp.exp(jnp.sum(lv[0:1] * lv[1:2], axis=-1, keepdims=True))
           - jnp.exp(jnp.sum(lv[2:3] * lv[3:4], axis=-1, keepdims=True)) + lambda_init)
    d = o[:tq] - lam * o[tq:]
    o_ref[0] = (_rms(d, sg_ref[...], SUBLN_EPS) * (1.0 - lambda_init)).astype(BF16)


def _rope_tables(T):
    half = HEAD // 2
    inv = 1.0 / (ROPE_THETA ** (jnp.arange(0, HEAD, 2, dtype=F32) / HEAD))
    ang = jnp.arange(T, dtype=F32)[:, None] * inv[None, :]
    cos, sin = jnp.cos(ang), jnp.sin(ang)
    reps = LANES // half
    cos_t = jnp.tile(cos, (1, reps))
    sin_t = jnp.tile(jnp.concatenate([-sin, sin], axis=1), (1, reps // 2))
    return cos_t, sin_t


def _diff_attention(x, B, T, g, w_qkv, q_gain, k_gain, lam_vec, subln_g, lambda_init, *, tm=256, tq=256):
    M, C = x.shape
    H = C // LANES
    cos_t, sin_t = _rope_tables(T)
    reps = LANES // HEAD
    qg = (jnp.tile(q_gain, reps) * HEAD ** -0.5).reshape(1, LANES)
    kg = jnp.tile(k_gain, reps).reshape(1, LANES)
    row = lambda w: pl.BlockSpec((tm, w), lambda i: (i, 0))
    table = pl.BlockSpec((tm, LANES), lambda i: (i % (T // tm), 0))
    q, k, v = pl.pallas_call(
        _qkv_kernel,
        grid=(M // tm,),
        in_specs=[row(C), _resident((1, C)), _resident((C, 3 * C)), _resident((1, LANES)), _resident((1, LANES)),
                  table, table],
        out_specs=[row(C), row(C), row(C)],
        out_shape=[jax.ShapeDtypeStruct((M, C), BF16)] * 3,
        compiler_params=_params("parallel"),
        name="attn_qkv",
    )(x, g.reshape(1, C), w_qkv, qg, kg, cos_t, sin_t)
    q, k, v = (z.reshape(B, T, C) for z in (q, k, v))
    qspec = pl.BlockSpec((1, tq, LANES), lambda b, h, i: (b, i, h))
    kvspec = pl.BlockSpec((1, T, LANES), lambda b, h, i: (b, 0, h))
    o = pl.pallas_call(
        functools.partial(_attn_kernel, tq=tq, lambda_init=lambda_init),
        grid=(B, H, T // tq),
        in_specs=[qspec, kvspec, kvspec, _resident((4, HEAD)), _resident((1, LANES))],
        out_specs=qspec,
        out_shape=jax.ShapeDtypeStruct((B, T, C), BF16),
        compiler_params=_params("parallel", "parallel", "arbitrary"),
        name="attn_core",
    )(q, k, v, lam_vec, subln_g.reshape(1, LANES))
    return o.reshape(M, C)


def _rwkv_prep_kernel(x_ref, xp_ref, g_ref, mix_ref, wrkv_ref, w1_ref, w2_ref, a1_ref, a2_ref, g1_ref, g2_ref,
                      w0_ref, a0_ref, kk_ref, ka_ref,
                      r_out, lw_out, k_out, v_out, an_out, bn_out, g_out, *, tiles_per_seq):
    i = pl.program_id(0)
    tm, C = x_ref.shape
    g = g_ref[...]
    h = _rms(x_ref[...], g, NORM_EPS)
    h_last = _rms(xp_ref[SUBLANES - 1:SUBLANES, :], g, NORM_EPS)
    h_last = jnp.where(i % tiles_per_seq == 0, 0.0, h_last)
    row = lax.broadcasted_iota(jnp.int32, h.shape, 0)
    dx = jnp.where(row == 0, h_last, pltpu.roll(h, 1, 0)) - h
    mix = mix_ref[...]

    def mixed(n):
        return (h + dx * mix[n:n + 1]).astype(BF16)

    r = _bdot(mixed(0), wrkv_ref[:, 0:C])
    k = _bdot(mixed(2), wrkv_ref[:, C:2 * C])
    v = _bdot(mixed(3), wrkv_ref[:, 2 * C:])
    w_lora = _bdot(jnp.tanh(_bdot(mixed(1), w1_ref[...])).astype(BF16), w2_ref[...])
    a_lora = _bdot(_bdot(mixed(4), a1_ref[...]).astype(BF16), a2_ref[...])
    gate = _bdot(jax.nn.sigmoid(_bdot(mixed(5), g1_ref[...])).astype(BF16), g2_ref[...])

    z = -(w0_ref[...] + w_lora)
    softplus = jnp.maximum(z, 0.0) + jnp.log1p(jnp.exp(-jnp.abs(z)))
    lw = -jnp.exp(-softplus - 0.5)
    a = jax.nn.sigmoid(a0_ref[...] + a_lora)
    kk = k * kk_ref[...]
    k2 = k * (1.0 + (a - 1.0) * ka_ref[...])

    r_out[...] = r
    lw_out[...] = lw
    k_out[...] = k2
    v_out[...] = v
    g_out[...] = gate
    lane = lax.broadcasted_iota(jnp.int32, (tm, LANES), 1)
    first = lane < HEAD
    for s in range(C // LANES):
        sl = slice(s * LANES, (s + 1) * LANES)
        kks = kk[:, sl]
        kkn = kks * lax.rsqrt(jnp.maximum(_half_sums(kks * kks, first), 1e-24))
        an_out[:, sl] = -kkn
        bn_out[:, sl] = kkn * a[:, sl]


def _rwkv_scan_kernel(r_ref, lw_ref, k_ref, v_ref, a_ref, b_ref, g_ref, rk_ref, lg_ref, lb_ref, o_ref, h_ref,
                      *, n_chunks):
    L = CHUNK

    @pl.when(pl.program_id(2) == 0)
    def _():
        h_ref[...] = jnp.zeros_like(h_ref)

    row = lax.broadcasted_iota(jnp.int32, (2 * L, LANES), 0)
    col = lax.broadcasted_iota(jnp.int32, (2 * L, LANES), 1)
    block_diag = (row >= L) == (col >= L)
    rt = row & (L - 1)
    ct = col & (L - 1)
    strict = ct < rt
    incl = ct <= rt
    eye = (row == col).astype(F32)
    lane = lax.broadcasted_iota(jnp.int32, (L, LANES), 1)
    first = lane < HEAD
    tr = lax.broadcasted_iota(jnp.int32, (L, L), 0)
    tc = lax.broadcasted_iota(jnp.int32, (L, L), 1)
    tri = (tc <= tr).astype(BF16)

    def stack(z):
        return jnp.concatenate([jnp.where(first, z, 0.0), jnp.where(first, 0.0, z)], axis=0)

    def split3(z):
        hi = z.astype(BF16)
        r1 = z - hi.astype(F32)
        mid = r1.astype(BF16)
        return hi, mid, (r1 - mid.astype(F32)).astype(BF16)

    rk = rk_ref[...]
    lg = lg_ref[...]
    lb = lb_ref[...]

    for c in range(n_chunks):
        sl = pl.ds(c * L, L)
        R = r_ref[0, sl, :]
        LW = lw_ref[0, sl, :]
        K = k_ref[0, sl, :]
        V = v_ref[0, sl, :]
        A = a_ref[0, sl, :]
        Bv = b_ref[0, sl, :]
        cs = _bdot(tri, jnp.concatenate(split3(LW), axis=1))
        cum = cs[:, :LANES] + cs[:, LANES:2 * LANES] + cs[:, 2 * LANES:]
        cum_last = cum[L - 1:L, :]
        dec = jnp.exp(cum)
        inv = jnp.exp(-cum)
        to_end = jnp.exp(cum_last - cum)
        Rt = R * dec
        At = A * jnp.exp(cum - LW)
        Kt = K * inv
        Bt = Bv * inv
        Vb = V.astype(BF16)

        AtS, RtS = stack(At), stack(Rt)
        bk = jnp.concatenate([Bt, Kt], axis=0).astype(BF16)
        kb = jnp.concatenate([Kt, Bt], axis=0).astype(BF16)
        p0 = _dot_nt(jnp.concatenate([AtS[:L], RtS[:L]], axis=0).astype(BF16), bk)
        p1 = _dot_nt(jnp.concatenate([AtS[L:], RtS[L:]], axis=0).astype(BF16), kb)
        top = jnp.concatenate([p0[:L], p1[:L]], axis=0)
        bot = jnp.concatenate([p0[L:], p1[L:]], axis=0)
        n_ab = jnp.where(block_diag & strict, top, 0.0)
        m_ak = jnp.where(block_diag | ~strict, 0.0, top)
        m_rb = jnp.where(block_diag & incl, bot, 0.0)
        m_rk = jnp.where(block_diag | ~incl, 0.0, bot)

        xb = n_ab.astype(BF16)
        tm_ = eye + n_ab
        xb = _bdot(xb, xb).astype(BF16)
        for _ in range(4):
            xt = _bdot(xb, jnp.concatenate([xb, tm_.astype(BF16)], axis=1))
            xb = xt[:, :LANES].astype(BF16)
            tm_ = tm_ + xt[:, LANES:]
        tm_ = tm_ + _bdot(xb, tm_.astype(BF16))

        vv = jnp.concatenate([Vb, Vb], axis=0)
        zy = _bdot(jnp.concatenate([m_ak, m_rk], axis=0).astype(BF16), vv)
        z_ak = jnp.where(block_diag, zy[:2 * L], 0.0)
        y_rk = jnp.where(block_diag, zy[2 * L:], 0.0)
        wu = _bdot(tm_.astype(BF16), jnp.concatenate([AtS, z_ak], axis=1).astype(BF16))
        wub = wu.astype(BF16)
        qy = _bdot(m_rb.astype(BF16), wub)
        q_eff = RtS + qy[:, :LANES]
        y_loc = y_rk + qy[:, LANES:]
        mg = _dot_tn(stack(Bv * to_end).astype(BF16), wub)
        m_c = eye * jnp.exp(cum_last) + mg[:, :LANES]
        g_c = mg[:, LANES:] + _dot_tn(stack(K * to_end).astype(BF16), stack(V).astype(BF16))

        hb = h_ref[...].astype(BF16)
        ys = _bdot(q_eff.astype(BF16), hb) + y_loc
        h_ref[...] = _bdot(m_c.astype(BF16), hb) + g_c
        y = ys[:L] + ys[L:]

        mu = _half_sums(y, first) * (1.0 / HEAD)
        d = y - mu
        var = _half_sums(d * d, first) * (1.0 / HEAD)
        yn = d * lax.rsqrt(var + RW_LNX_EPS) * lg + lb
        bonus = _half_sums(R * K * rk, first) * V
        o_ref[0, sl, :] = ((yn + bonus) * g_ref[0, sl, :]).astype(BF16)


def _rwkv_time_mix(x, B, T, g, mix, w_rkv, w0, w1, w2, a0, a1, a2, g1, g2, k_k, k_a, r_k, lnx_g, lnx_b,
                   *, tm=256, tt=256):
    M, C = x.shape
    row = pl.BlockSpec((tm, C), lambda i: (i, 0))
    prev = pl.BlockSpec((SUBLANES, C), lambda i: (jnp.maximum(i * (tm // SUBLANES) - 1, 0), 0))
    vec = lambda a: a.reshape(1, C)
    outs = pl.pallas_call(
        functools.partial(_rwkv_prep_kernel, tiles_per_seq=T // tm),
        grid=(M // tm,),
        in_specs=[row, prev, _resident((1, C)), _resident(mix.shape), _resident(w_rkv.shape),
                  _resident(w1.shape), _resident(w2.shape), _resident(a1.shape), _resident(a2.shape),
                  _resident(g1.shape), _resident(g2.shape)] + [_resident((1, C))] * 4,
        out_specs=[row] * 7,
        out_shape=[jax.ShapeDtypeStruct((M, C), F32)] * 7,
        compiler_params=_params("parallel"),
        name="rwkv_prep",
    )(x, x, vec(g), mix, w_rkv, w1, w2, a1, a2, g1, g2, vec(w0), vec(a0), vec(k_k), vec(k_a))
    seq = [z.reshape(B, T, C) for z in outs]
    slab = pl.BlockSpec((1, tt, LANES), lambda b, s, t: (b, t, s))
    pvec = pl.BlockSpec((1, LANES), lambda b, s, t: (0, s))
    y = pl.pallas_call(
        functools.partial(_rwkv_scan_kernel, n_chunks=tt // CHUNK),
        grid=(B, C // LANES, T // tt),
        in_specs=[slab] * 7 + [pvec] * 3,
        out_specs=slab,
        out_shape=jax.ShapeDtypeStruct((B, T, C), BF16),
        scratch_shapes=[pltpu.VMEM((LANES, LANES), F32)],
        compiler_params=_params("parallel", "parallel", "arbitrary"),
        name="rwkv_scan",
    )(*seq, vec(r_k), vec(lnx_g), vec(lnx_b))
    return y.reshape(M, C)


def _pool_kernel(x_ref, xh_ref, g_ref, o_ref):
    i = pl.program_id(1)
    tt, C = x_ref.shape[1], x_ref.shape[2]
    group = C // len(POOL_WINDOWS)
    g = g_ref[...]
    h = _rms(x_ref[0], g, NORM_EPS)
    halo = jnp.where(i == 0, 0.0, _rms(xh_ref[0], g, NORM_EPS))
    he = jnp.concatenate([halo, h], axis=0)
    t = i * tt + lax.broadcasted_iota(jnp.int32, (tt, 1), 0)
    for gi, win in enumerate(POOL_WINDOWS):
        sl = slice(gi * group, (gi + 1) * group)
        s = he[:, sl]
        shift = 1
        while shift < win:
            s = s + pltpu.roll(s, shift, 0)
            shift *= 2
        cnt = jnp.minimum(t + 1, win).astype(F32)
        o_ref[0, :, sl] = (s[POOL_HALO:] / cnt - h[:, sl]).astype(BF16)


def _multiscale_pool(x, B, T, g, *, tt=512):
    M, C = x.shape
    x3 = x.reshape(B, T, C)
    blocks = tt // POOL_HALO
    d = pl.pallas_call(
        _pool_kernel,
        grid=(B, T // tt),
        in_specs=[pl.BlockSpec((1, tt, C), lambda b, i: (b, i, 0)),
                  pl.BlockSpec((1, POOL_HALO, C), lambda b, i: (b, jnp.maximum(i * blocks - 1, 0), 0)),
                  _resident((1, C))],
        out_specs=pl.BlockSpec((1, tt, C), lambda b, i: (b, i, 0)),
        out_shape=jax.ShapeDtypeStruct((B, T, C), BF16),
        compiler_params=_params("parallel", "arbitrary"),
        name="pool",
    )(x3, x3, g.reshape(1, C))
    return d.reshape(M, C)


def kernel(x, mix_norm, ffn_norm, ffn_w_in, ffn_w_out, da_wqkv, da_wo, da_q_gain, da_k_gain, da_lambda, da_subln, rw_mix, rw_wrkv, rw_wo, rw_w0, rw_w1, rw_w2, rw_a0, rw_a1, rw_a2, rw_g1, rw_g2, rw_kk, rw_ka, rw_rk, rw_lnx_g, rw_lnx_b, pool_w, pool_scale):
    B, T, C = x.shape
    depth = mix_norm.shape[0]
    bf = lambda w: w.astype(BF16)
    xs = x.reshape(B * T, C)
    ia = ir = ip = 0
    for layer in range(depth):
        kind = layer % N_MIXERS
        scale = None
        if kind == 0:
            lambda_init = 0.8 - 0.6 * math.exp(-0.3 * layer)
            y = _diff_attention(xs, B, T, mix_norm[layer], bf(da_wqkv[ia]), da_q_gain[ia], da_k_gain[ia],
                                da_lambda[ia], da_subln[ia], lambda_init)
            w_mix = bf(da_wo[ia])
            ia += 1
        elif kind == 1:
            y = _rwkv_time_mix(xs, B, T, mix_norm[layer], rw_mix[ir], bf(rw_wrkv[ir]), rw_w0[ir], bf(rw_w1[ir]),
                               bf(rw_w2[ir]), rw_a0[ir], bf(rw_a1[ir]), bf(rw_a2[ir]), bf(rw_g1[ir]),
                               bf(rw_g2[ir]), rw_kk[ir], rw_ka[ir], rw_rk[ir], rw_lnx_g[ir], rw_lnx_b[ir])
            w_mix = bf(rw_wo[ir])
            ir += 1
        else:
            y = _multiscale_pool(xs, B, T, mix_norm[layer])
            w_mix = bf(jax.scipy.linalg.block_diag(*pool_w[ip]))
            scale = pool_scale[ip]
            ip += 1
        xs = _mix_ffn(xs, y, w_mix, scale, ffn_norm[layer], bf(ffn_w_in[layer]), bf(ffn_w_out[layer]))
    return xs.reshape(B, T, C)
```

```python
import functools
import math

import jax
import jax.numpy as jnp
from jax import lax
from jax.experimental import pallas as pl
from jax.experimental.pallas import tpu as pltpu

F32 = jnp.float32
BF16 = jnp.bfloat16

LANES = 128
SUBLANES = 8
VMEM_LIMIT_BYTES = 48 * 1024 * 1024

CHUNK = 64
HEAD = 64
N_MIXERS = 3
NORM_EPS = 1e-6
QK_EPS = 1e-6
SUBLN_EPS = 1e-5
ROPE_THETA = 10000.0
RW_LNX_EPS = 64e-5
POOL_WINDOWS = (2, 4, 8, 16)
POOL_HALO = 16
NEG = -1e30


def _rms(z, gain, eps):
    return z * lax.rsqrt(jnp.mean(z * z, axis=-1, keepdims=True) + eps) * gain


def _bdot(a, b):
    return jnp.dot(a, b, preferred_element_type=F32)


def _dot_nt(a, b):
    return lax.dot_general(a, b, (((1,), (1,)), ((), ())), preferred_element_type=F32)


def _dot_tn(a, b):
    return lax.dot_general(a, b, (((0,), (0,)), ((), ())), preferred_element_type=F32)


def _half_sums(z, first):
    lo = jnp.sum(jnp.where(first, z, 0.0), axis=-1, keepdims=True)
    hi = jnp.sum(jnp.where(first, 0.0, z), axis=-1, keepdims=True)
    return jnp.where(first, lo, hi)


def _params(*sem):
    return pltpu.CompilerParams(dimension_semantics=sem, vmem_limit_bytes=VMEM_LIMIT_BYTES)


def _resident(shape):
    return pl.BlockSpec(shape, lambda *_: (0,) * len(shape), pipeline_mode=pl.Buffered(1))


def _mix_ffn_kernel(*refs, hidden, tf, has_scale):
    if has_scale:
        x_ref, y_ref, wm_ref, sc_ref, g_ref, win_ref, wout_ref, o_ref = refs
    else:
        x_ref, y_ref, wm_ref, g_ref, win_ref, wout_ref, o_ref = refs
    y = _bdot(y_ref[...], wm_ref[...])
    if has_scale:
        y = y * sc_ref[...]
    x1 = x_ref[...] + y
    hn = _rms(x1, g_ref[...], NORM_EPS).astype(BF16)
    acc = x1
    for f in range(hidden // tf):
        gate = _bdot(hn, win_ref[:, f * tf:(f + 1) * tf])
        up = _bdot(hn, win_ref[:, hidden + f * tf:hidden + (f + 1) * tf])
        act = (gate * jax.nn.sigmoid(gate) * up).astype(BF16)
        acc = acc + _bdot(act, wout_ref[f * tf:(f + 1) * tf, :])
    o_ref[...] = acc


def _mix_ffn(x, y, w_mix, scale, g, w_in, w_out, *, tm=512, tf=256):
    M, C = x.shape
    hidden = w_out.shape[0]
    has_scale = scale is not None
    row = lambda w: pl.BlockSpec((tm, w), lambda i: (i, 0))
    in_specs = [row(C), row(C), _resident((C, C))]
    args = [x, y, w_mix]
    if has_scale:
        in_specs.append(_resident((1, C)))
        args.append(scale.reshape(1, C))
    in_specs += [_resident((1, C)), _resident((C, 2 * hidden)), _resident((hidden, C))]
    args += [g.reshape(1, C), w_in, w_out]
    return pl.pallas_call(
        functools.partial(_mix_ffn_kernel, hidden=hidden, tf=tf, has_scale=has_scale),
        grid=(M // tm,),
        in_specs=in_specs,
        out_specs=row(C),
        out_shape=jax.ShapeDtypeStruct((M, C), F32),
        compiler_params=_params("parallel"),
        name="mix_ffn",
    )(*args)


def _qkv_kernel(x_ref, g_ref, w_ref, qg_ref, kg_ref, cos_ref, sin_ref, qt_ref, k_ref, vt_ref):
    C = x_ref.shape[1]
    hn = _rms(x_ref[...], g_ref[...], NORM_EPS).astype(BF16)
    qkv = _bdot(hn, w_ref[...])
    cos = cos_ref[...]
    sin = sin_ref[...]
    lane = lax.broadcasted_iota(jnp.int32, cos.shape, 1)
    first = lane < HEAD
    low_half = (lane & (HEAD - 1)) < HEAD // 2

    def norm_rope(z, gain):
        ms = _half_sums(z * z, first) * (1.0 / HEAD)
        zn = z * lax.rsqrt(ms + QK_EPS) * gain
        partner = jnp.where(low_half, pltpu.roll(zn, LANES - HEAD // 2, 1), pltpu.roll(zn, HEAD // 2, 1))
        return zn * cos + partner * sin

    for s in range(C // LANES):
        sl = slice(s * LANES, (s + 1) * LANES)
        qt_ref[0, 0, sl, :] = norm_rope(qkv[:, sl], qg_ref[...]).T.astype(BF16)
        k_ref[:, sl] = norm_rope(qkv[:, C + s * LANES:C + (s + 1) * LANES], kg_ref[...]).astype(BF16)
        vt_ref[0, 0, sl, :] = qkv[:, 2 * C + s * LANES:2 * C + (s + 1) * LANES].T.astype(BF16)


def _attn_kernel(qt_ref, k_ref, vt_ref, lam_ref, sg_ref, o_ref, *, tq, lambda_init):
    i = pl.program_id(2)
    qt = qt_ref[0, 0]
    feat = lax.broadcasted_iota(jnp.int32, qt.shape, 0)
    zero = jnp.zeros_like(qt)
    qst = jnp.concatenate([jnp.where(feat < HEAD, qt, zero), jnp.where(feat < HEAD, zero, qt)], axis=1)

    def tile(j, carry, diagonal):
        m, l, acc = carry
        start = pl.multiple_of(j * tq, tq)
        s = _bdot(k_ref[0, pl.ds(start, tq), :], qst)
        if diagonal:
            key = lax.broadcasted_iota(jnp.int32, s.shape, 0)
            qry = lax.broadcasted_iota(jnp.int32, s.shape, 1)
            qry = jnp.where(qry >= tq, qry - tq, qry)
            s = jnp.where((key // CHUNK) <= (qry // CHUNK), s, NEG)
        m_new = jnp.maximum(m, jnp.max(s, axis=0, keepdims=True))
        alpha = jnp.exp2(m - m_new)
        p = jnp.exp2(s - m_new)
        l = alpha * l + jnp.sum(p, axis=0, keepdims=True)
        acc = alpha * acc + _bdot(vt_ref[0, j], p.astype(BF16))
        return m_new, l, acc

    carry = (jnp.full((1, 2 * tq), NEG, F32), jnp.zeros((1, 2 * tq), F32), jnp.zeros((LANES, 2 * tq), F32))
    carry = lax.fori_loop(0, i, lambda j, c: tile(j, c, False), carry)
    _, l, acc = tile(i, carry, True)
    o = acc / l
    lv = lam_ref[...]
    lam = (jnp.exp(jnp.sum(lv[0:1] * lv[1:2], axis=-1, keepdims=True))
           - jnp.exp(jnp.sum(lv[2:3] * lv[3:4], axis=-1, keepdims=True)) + lambda_init)
    d = (o[:, :tq] - lam * o[:, tq:]).T
    o_ref[0] = (_rms(d, sg_ref[...], SUBLN_EPS) * (1.0 - lambda_init)).astype(BF16)


def _rope_tables(T):
    half = HEAD // 2
    inv = 1.0 / (ROPE_THETA ** (jnp.arange(0, HEAD, 2, dtype=F32) / HEAD))
    ang = jnp.arange(T, dtype=F32)[:, None] * inv[None, :]
    cos, sin = jnp.cos(ang), jnp.sin(ang)
    reps = LANES // half
    cos_t = jnp.tile(cos, (1, reps))
    sin_t = jnp.tile(jnp.concatenate([-sin, sin], axis=1), (1, reps // 2))
    return cos_t, sin_t


def _diff_attention(x, B, T, g, w_qkv, q_gain, k_gain, lam_vec, subln_g, lambda_init, *, tq=512):
    M, C = x.shape
    H = C // LANES
    nt = T // tq
    cos_t, sin_t = _rope_tables(T)
    reps = LANES // HEAD
    qg = (jnp.tile(q_gain, reps) * (HEAD ** -0.5 * math.log2(math.e))).reshape(1, LANES)
    kg = jnp.tile(k_gain, reps).reshape(1, LANES)
    row = pl.BlockSpec((tq, C), lambda i: (i, 0))
    table = pl.BlockSpec((tq, LANES), lambda i: (i % nt, 0))
    transposed = pl.BlockSpec((1, 1, C, tq), lambda i: (i // nt, i % nt, 0, 0))
    qt, k, vt = pl.pallas_call(
        _qkv_kernel,
        grid=(M // tq,),
        in_specs=[row, _resident((1, C)), _resident((C, 3 * C)), _resident((1, LANES)), _resident((1, LANES)),
                  table, table],
        out_specs=[transposed, row, transposed],
        out_shape=[jax.ShapeDtypeStruct((B, nt, C, tq), BF16), jax.ShapeDtypeStruct((M, C), BF16),
                   jax.ShapeDtypeStruct((B, nt, C, tq), BF16)],
        compiler_params=_params("parallel"),
        name="attn_qkv",
    )(x, g.reshape(1, C), w_qkv, qg, kg, cos_t, sin_t)
    o = pl.pallas_call(
        functools.partial(_attn_kernel, tq=tq, lambda_init=lambda_init),
        grid=(B, H, nt),
        in_specs=[pl.BlockSpec((1, 1, LANES, tq), lambda b, h, i: (b, i, h, 0)),
                  pl.BlockSpec((1, T, LANES), lambda b, h, i: (b, 0, h)),
                  pl.BlockSpec((1, nt, LANES, tq), lambda b, h, i: (b, 0, h, 0)),
                  _resident((4, HEAD)), _resident((1, LANES))],
        out_specs=pl.BlockSpec((1, tq, LANES), lambda b, h, i: (b, i, h)),
        out_shape=jax.ShapeDtypeStruct((B, T, C), BF16),
        compiler_params=_params("parallel", "parallel", "arbitrary"),
        name="attn_core",
    )(qt, k.reshape(B, T, C), vt, lam_vec, subln_g.reshape(1, LANES))
    return o.reshape(M, C)


def _rwkv_prep_kernel(x_ref, xp_ref, g_ref, mix_ref, wrkv_ref, w1_ref, w2_ref, a1_ref, a2_ref, g1_ref, g2_ref,
                      w0_ref, a0_ref, kk_ref, ka_ref,
                      r_out, lw_out, k_out, v_out, an_out, bn_out, g_out, *, tiles_per_seq):
    i = pl.program_id(0)
    tm, C = x_ref.shape
    g = g_ref[...]
    h = _rms(x_ref[...], g, NORM_EPS)
    h_last = _rms(xp_ref[SUBLANES - 1:SUBLANES, :], g, NORM_EPS)
    h_last = jnp.where(i % tiles_per_seq == 0, 0.0, h_last)
    row = lax.broadcasted_iota(jnp.int32, h.shape, 0)
    dx = jnp.where(row == 0, h_last, pltpu.roll(h, 1, 0)) - h
    mix = mix_ref[...]

    def mixed(n):
        return (h + dx * mix[n:n + 1]).astype(BF16)

    r = _bdot(mixed(0), wrkv_ref[:, 0:C])
    k = _bdot(mixed(2), wrkv_ref[:, C:2 * C])
    v = _bdot(mixed(3), wrkv_ref[:, 2 * C:])
    w_lora = _bdot(jnp.tanh(_bdot(mixed(1), w1_ref[...])).astype(BF16), w2_ref[...])
    a_lora = _bdot(_bdot(mixed(4), a1_ref[...]).astype(BF16), a2_ref[...])
    gate = _bdot(jax.nn.sigmoid(_bdot(mixed(5), g1_ref[...])).astype(BF16), g2_ref[...])

    z = -(w0_ref[...] + w_lora)
    softplus = jnp.maximum(z, 0.0) + jnp.log1p(jnp.exp(-jnp.abs(z)))
    lw = -jnp.exp(-softplus - 0.5)
    a = jax.nn.sigmoid(a0_ref[...] + a_lora)
    kk = k * kk_ref[...]
    k2 = k * (1.0 + (a - 1.0) * ka_ref[...])

    r_out[...] = r
    lw_out[...] = lw
    k_out[...] = k2
    v_out[...] = v
    g_out[...] = gate
    lane = lax.broadcasted_iota(jnp.int32, (tm, LANES), 1)
    first = lane < HEAD
    for s in range(C // LANES):
        sl = slice(s * LANES, (s + 1) * LANES)
        kks = kk[:, sl]
        kkn = kks * lax.rsqrt(jnp.maximum(_half_sums(kks * kks, first), 1e-24))
        an_out[:, sl] = -kkn
        bn_out[:, sl] = kkn * a[:, sl]


def _rwkv_scan_kernel(r_ref, lw_ref, k_ref, v_ref, a_ref, b_ref, g_ref, rk_ref, lg_ref, lb_ref, o_ref, h_ref,
                      *, n_chunks):
    L = CHUNK

    @pl.when(pl.program_id(2) == 0)
    def _():
        h_ref[...] = jnp.zeros_like(h_ref)

    row = lax.broadcasted_iota(jnp.int32, (2 * L, LANES), 0)
    col = lax.broadcasted_iota(jnp.int32, (2 * L, LANES), 1)
    block_diag = (row >= L) == (col >= L)
    rt = row & (L - 1)
    ct = col & (L - 1)
    strict = ct < rt
    incl = ct <= rt
    eye = (row == col).astype(F32)
    lane = lax.broadcasted_iota(jnp.int32, (L, LANES), 1)
    first = lane < HEAD
    tr = lax.broadcasted_iota(jnp.int32, (L, L), 0)
    tc = lax.broadcasted_iota(jnp.int32, (L, L), 1)
    tri = (tc <= tr).astype(BF16)

    def stack(z):
        return jnp.concatenate([jnp.where(first, z, 0.0), jnp.where(first, 0.0, z)], axis=0)

    def split3(z):
        hi = z.astype(BF16)
        r1 = z - hi.astype(F32)
        mid = r1.astype(BF16)
        return hi, mid, (r1 - mid.astype(F32)).astype(BF16)

    rk = rk_ref[...]
    lg = lg_ref[...]
    lb = lb_ref[...]

    chunks = range(n_chunks)
    sls = [pl.ds(c * L, L) for c in chunks]
    R = [r_ref[0, sl, :] for sl in sls]
    LW = [lw_ref[0, sl, :] for sl in sls]
    K = [k_ref[0, sl, :] for sl in sls]
    V = [v_ref[0, sl, :] for sl in sls]
    A = [a_ref[0, sl, :] for sl in sls]
    Bv = [b_ref[0, sl, :] for sl in sls]
    cs = [_bdot(tri, jnp.concatenate(split3(LW[c]), axis=1)) for c in chunks]
    cum = [z[:, :LANES] + z[:, LANES:2 * LANES] + z[:, 2 * LANES:] for z in cs]
    cum_last = [z[L - 1:L, :] for z in cum]
    inv = [jnp.exp(-z) for z in cum]
    to_end = [jnp.exp(cum_last[c] - cum[c]) for c in chunks]
    Rt = [R[c] * jnp.exp(cum[c]) for c in chunks]
    At = [A[c] * jnp.exp(cum[c] - LW[c]) for c in chunks]
    Kt = [K[c] * inv[c] for c in chunks]
    Bt = [Bv[c] * inv[c] for c in chunks]
    AtS = [stack(z) for z in At]
    RtS = [stack(z) for z in Rt]

    p0 = [_dot_nt(jnp.concatenate([AtS[c][:L], RtS[c][:L]], axis=0).astype(BF16),
                  jnp.concatenate([Bt[c], Kt[c]], axis=0).astype(BF16)) for c in chunks]
    p1 = [_dot_nt(jnp.concatenate([AtS[c][L:], RtS[c][L:]], axis=0).astype(BF16),
                  jnp.concatenate([Kt[c], Bt[c]], axis=0).astype(BF16)) for c in chunks]
    top = [jnp.concatenate([p0[c][:L], p1[c][:L]], axis=0) for c in chunks]
    bot = [jnp.concatenate([p0[c][L:], p1[c][L:]], axis=0) for c in chunks]
    n_ab = [jnp.where(block_diag & strict, z, 0.0) for z in top]
    m_ak = [jnp.where(block_diag | ~strict, 0.0, z) for z in top]
    m_rb = [jnp.where(block_diag & incl, z, 0.0) for z in bot]
    m_rk = [jnp.where(block_diag | ~incl, 0.0, z) for z in bot]

    tinv = [eye + z for z in n_ab]
    xb = [z.astype(BF16) for z in n_ab]
    xb = [_bdot(z, z).astype(BF16) for z in xb]
    for _ in range(4):
        xt = [_bdot(xb[c], jnp.concatenate([xb[c], tinv[c].astype(BF16)], axis=1)) for c in chunks]
        xb = [z[:, :LANES].astype(BF16) for z in xt]
        tinv = [tinv[c] + xt[c][:, LANES:] for c in chunks]
    tinv = [tinv[c] + _bdot(xb[c], tinv[c].astype(BF16)) for c in chunks]

    Vb = [z.astype(BF16) for z in V]
    zy = [_bdot(jnp.concatenate([m_ak[c], m_rk[c]], axis=0).astype(BF16),
                jnp.concatenate([Vb[c], Vb[c]], axis=0)) for c in chunks]
    z_ak = [jnp.where(block_diag, z[:2 * L], 0.0) for z in zy]
    y_rk = [jnp.where(block_diag, z[2 * L:], 0.0) for z in zy]
    wub = [_bdot(tinv[c].astype(BF16), jnp.concatenate([AtS[c], z_ak[c]], axis=1).astype(BF16)).astype(BF16)
           for c in chunks]
    qy = [_bdot(m_rb[c].astype(BF16), wub[c]) for c in chunks]
    q_eff = [(RtS[c] + qy[c][:, :LANES]).astype(BF16) for c in chunks]
    y_loc = [y_rk[c] + qy[c][:, LANES:] for c in chunks]
    mg = [_dot_tn(stack(Bv[c] * to_end[c]).astype(BF16), wub[c]) for c in chunks]
    m_c = [(eye * jnp.exp(cum_last[c]) + mg[c][:, :LANES]).astype(BF16) for c in chunks]
    g_c = [mg[c][:, LANES:] + _dot_tn(stack(K[c] * to_end[c]).astype(BF16), stack(V[c]).astype(BF16))
           for c in chunks]

    h = h_ref[...]
    ys = []
    for c in chunks:
        hb = h.astype(BF16)
        ys.append(_bdot(q_eff[c], hb) + y_loc[c])
        h = _bdot(m_c[c], hb) + g_c[c]
    h_ref[...] = h

    for c in chunks:
        y = ys[c][:L] + ys[c][L:]
        mu = _half_sums(y, first) * (1.0 / HEAD)
        d = y - mu
        var = _half_sums(d * d, first) * (1.0 / HEAD)
        yn = d * lax.rsqrt(var + RW_LNX_EPS) * lg + lb
        bonus = _half_sums(R[c] * K[c] * rk, first) * V[c]
        o_ref[0, sls[c], :] = ((yn + bonus) * g_ref[0, sls[c], :]).astype(BF16)


def _rwkv_time_mix(x, B, T, g, mix, w_rkv, w0, w1, w2, a0, a1, a2, g1, g2, k_k, k_a, r_k, lnx_g, lnx_b,
                   *, tm=256, tt=1024):
    M, C = x.shape
    row = pl.BlockSpec((tm, C), lambda i: (i, 0))
    prev = pl.BlockSpec((SUBLANES, C), lambda i: (jnp.maximum(i * (tm // SUBLANES) - 1, 0), 0))
    vec = lambda a: a.reshape(1, C)
    outs = pl.pallas_call(
        functools.partial(_rwkv_prep_kernel, tiles_per_seq=T // tm),
        grid=(M // tm,),
        in_specs=[row, prev, _resident((1, C)), _resident(mix.shape), _resident(w_rkv.shape),
                  _resident(w1.shape), _resident(w2.shape), _resident(a1.shape), _resident(a2.shape),
                  _resident(g1.shape), _resident(g2.shape)] + [_resident((1, C))] * 4,
        out_specs=[row] * 7,
        out_shape=[jax.ShapeDtypeStruct((M, C), F32)] * 7,
        compiler_params=_params("parallel"),
        name="rwkv_prep",
    )(x, x, vec(g), mix, w_rkv, w1, w2, a1, a2, g1, g2, vec(w0), vec(a0), vec(k_k), vec(k_a))
    seq = [z.reshape(B, T, C) for z in outs]
    slab = pl.BlockSpec((1, tt, LANES), lambda b, s, t: (b, t, s))
    pvec = pl.BlockSpec((1, LANES), lambda b, s, t: (0, s))
    y = pl.pallas_call(
        functools.partial(_rwkv_scan_kernel, n_chunks=tt // CHUNK),
        grid=(B, C // LANES, T // tt),
        in_specs=[slab] * 7 + [pvec] * 3,
        out_specs=slab,
        out_shape=jax.ShapeDtypeStruct((B, T, C), BF16),
        scratch_shapes=[pltpu.VMEM((LANES, LANES), F32)],
        compiler_params=_params("parallel", "parallel", "arbitrary"),
        name="rwkv_scan",
    )(*seq, vec(r_k), vec(lnx_g), vec(lnx_b))
    return y.reshape(M, C)


def _pool_kernel(x_ref, xh_ref, g_ref, o_ref):
    i = pl.program_id(1)
    tt, C = x_ref.shape[1], x_ref.shape[2]
    group = C // len(POOL_WINDOWS)
    g = g_ref[...]
    h = _rms(x_ref[0], g, NORM_EPS)
    halo = jnp.where(i == 0, 0.0, _rms(xh_ref[0], g, NORM_EPS))
    he = jnp.concatenate([halo, h], axis=0)
    t = i * tt + lax.broadcasted_iota(jnp.int32, (tt, 1), 0)
    for gi, win in enumerate(POOL_WINDOWS):
        sl = slice(gi * group, (gi + 1) * group)
        s = he[:, sl]
        shift = 1
        while shift < win:
            s = s + pltpu.roll(s, shift, 0)
            shift *= 2
        cnt = jnp.minimum(t + 1, win).astype(F32)
        o_ref[0, :, sl] = (s[POOL_HALO:] / cnt - h[:, sl]).astype(BF16)


def _multiscale_pool(x, B, T, g, *, tt=512):
    M, C = x.shape
    x3 = x.reshape(B, T, C)
    blocks = tt // POOL_HALO
    d = pl.pallas_call(
        _pool_kernel,
        grid=(B, T // tt),
        in_specs=[pl.BlockSpec((1, tt, C), lambda b, i: (b, i, 0)),
                  pl.BlockSpec((1, POOL_HALO, C), lambda b, i: (b, jnp.maximum(i * blocks - 1, 0), 0)),
                  _resident((1, C))],
        out_specs=pl.BlockSpec((1, tt, C), lambda b, i: (b, i, 0)),
        out_shape=jax.ShapeDtypeStruct((B, T, C), BF16),
        compiler_params=_params("parallel", "arbitrary"),
        name="pool",
    )(x3, x3, g.reshape(1, C))
    return d.reshape(M, C)


def kernel(x, mix_norm, ffn_norm, ffn_w_in, ffn_w_out, da_wqkv, da_wo, da_q_gain, da_k_gain, da_lambda, da_subln, rw_mix, rw_wrkv, rw_wo, rw_w0, rw_w1, rw_w2, rw_a0, rw_a1, rw_a2, rw_g1, rw_g2, rw_kk, rw_ka, rw_rk, rw_lnx_g, rw_lnx_b, pool_w, pool_scale):
    B, T, C = x.shape
    depth = mix_norm.shape[0]
    bf = lambda w: w.astype(BF16)
    xs = x.reshape(B * T, C)
    ia = ir = ip = 0
    for layer in range(depth):
        kind = layer % N_MIXERS
        scale = None
        if kind == 0:
            lambda_init = 0.8 - 0.6 * math.exp(-0.3 * layer)
            y = _diff_attention(xs, B, T, mix_norm[layer], bf(da_wqkv[ia]), da_q_gain[ia], da_k_gain[ia],
                                da_lambda[ia], da_subln[ia], lambda_init)
            w_mix = bf(da_wo[ia])
            ia += 1
        elif kind == 1:
            y = _rwkv_time_mix(xs, B, T, mix_norm[layer], rw_mix[ir], bf(rw_wrkv[ir]), rw_w0[ir], bf(rw_w1[ir]),
                               bf(rw_w2[ir]), rw_a0[ir], bf(rw_a1[ir]), bf(rw_a2[ir]), bf(rw_g1[ir]),
                               bf(rw_g2[ir]), rw_kk[ir], rw_ka[ir], rw_rk[ir], rw_lnx_g[ir], rw_lnx_b[ir])
            w_mix = bf(rw_wo[ir])
            ir += 1
        else:
            y = _multiscale_pool(xs, B, T, mix_norm[layer])
            w_mix = bf(jax.scipy.linalg.block_diag(*pool_w[ip]))
            scale = pool_scale[ip]
            ip += 1
        xs = _mix_ffn(xs, y, w_mix, scale, ffn_norm[layer], bf(ffn_w_in[layer]), bf(ffn_w_out[layer]))
    return xs.reshape(B, T, C)
```

```python
import functools
import math

import jax
import jax.numpy as jnp
from jax import lax
from jax.experimental import pallas as pl
from jax.experimental.pallas import tpu as pltpu

F32 = jnp.float32
BF16 = jnp.bfloat16

LANES = 128
SUBLANES = 8
VMEM_LIMIT_BYTES = 48 * 1024 * 1024

CHUNK = 64
HEAD = 64
N_MIXERS = 3
NORM_EPS = 1e-6
QK_EPS = 1e-6
SUBLN_EPS = 1e-5
ROPE_THETA = 10000.0
RW_LNX_EPS = 64e-5
POOL_WINDOWS = (2, 4, 8, 16)
POOL_HALO = 16
NEG = -1e30


def _rms(z, gain, eps):
    return z * lax.rsqrt(jnp.mean(z * z, axis=-1, keepdims=True) + eps) * gain


def _bdot(a, b):
    return jnp.dot(a, b, preferred_element_type=F32)


def _dot_nt(a, b):
    return lax.dot_general(a, b, (((1,), (1,)), ((), ())), preferred_element_type=F32)


def _dot_tn(a, b):
    return lax.dot_general(a, b, (((0,), (0,)), ((), ())), preferred_element_type=F32)


def _half_sums(z, first):
    lo = jnp.sum(jnp.where(first, z, 0.0), axis=-1, keepdims=True)
    hi = jnp.sum(jnp.where(first, 0.0, z), axis=-1, keepdims=True)
    return jnp.where(first, lo, hi)


def _params(*sem):
    return pltpu.CompilerParams(dimension_semantics=sem, vmem_limit_bytes=VMEM_LIMIT_BYTES)


def _resident(shape):
    return pl.BlockSpec(shape, lambda *_: (0,) * len(shape), pipeline_mode=pl.Buffered(1))


def _mix_ffn_kernel(*refs, hidden, tf, has_scale):
    if has_scale:
        x_ref, y_ref, wm_ref, sc_ref, g_ref, win_ref, wout_ref, o_ref = refs
    else:
        x_ref, y_ref, wm_ref, g_ref, win_ref, wout_ref, o_ref = refs
    y = _bdot(y_ref[...], wm_ref[...])
    if has_scale:
        y = y * sc_ref[...]
    x1 = x_ref[...] + y
    hn = _rms(x1, g_ref[...], NORM_EPS).astype(BF16)
    acc = x1
    for f in range(hidden // tf):
        gate = _bdot(hn, win_ref[:, f * tf:(f + 1) * tf])
        up = _bdot(hn, win_ref[:, hidden + f * tf:hidden + (f + 1) * tf])
        act = (gate * jax.nn.sigmoid(gate) * up).astype(BF16)
        acc = acc + _bdot(act, wout_ref[f * tf:(f + 1) * tf, :])
    o_ref[...] = acc


def _mix_ffn(x, y, w_mix, scale, g, w_in, w_out, *, tm=512, tf=256):
    M, C = x.shape
    hidden = w_out.shape[0]
    has_scale = scale is not None
    row = lambda w: pl.BlockSpec((tm, w), lambda i: (i, 0))
    in_specs = [row(C), row(C), _resident((C, C))]
    args = [x, y, w_mix]
    if has_scale:
        in_specs.append(_resident((1, C)))
        args.append(scale.reshape(1, C))
    in_specs += [_resident((1, C)), _resident((C, 2 * hidden)), _resident((hidden, C))]
    args += [g.reshape(1, C), w_in, w_out]
    return pl.pallas_call(
        functools.partial(_mix_ffn_kernel, hidden=hidden, tf=tf, has_scale=has_scale),
        grid=(M // tm,),
        in_specs=in_specs,
        out_specs=row(C),
        out_shape=jax.ShapeDtypeStruct((M, C), F32),
        compiler_params=_params("parallel"),
        name="mix_ffn",
    )(*args)


def _qkv_kernel(x_ref, g_ref, w_ref, qg_ref, kg_ref, cos_ref, sin_ref, qt_ref, k_ref, vt_ref):
    C = x_ref.shape[1]
    hn = _rms(x_ref[...], g_ref[...], NORM_EPS).astype(BF16)
    qkv = _bdot(hn, w_ref[...])
    cos = cos_ref[...]
    sin = sin_ref[...]
    lane = lax.broadcasted_iota(jnp.int32, cos.shape, 1)
    first = lane < HEAD
    low_half = (lane & (HEAD - 1)) < HEAD // 2

    def norm_rope(z, gain):
        ms = _half_sums(z * z, first) * (1.0 / HEAD)
        zn = z * lax.rsqrt(ms + QK_EPS) * gain
        partner = jnp.where(low_half, pltpu.roll(zn, LANES - HEAD // 2, 1), pltpu.roll(zn, HEAD // 2, 1))
        return zn * cos + partner * sin

    for s in range(C // LANES):
        sl = slice(s * LANES, (s + 1) * LANES)
        qt_ref[0, 0, sl, :] = norm_rope(qkv[:, sl], qg_ref[...]).T.astype(BF16)
        k_ref[:, sl] = norm_rope(qkv[:, C + s * LANES:C + (s + 1) * LANES], kg_ref[...]).astype(BF16)
        vt_ref[0, 0, sl, :] = qkv[:, 2 * C + s * LANES:2 * C + (s + 1) * LANES].T.astype(BF16)


def _attn_kernel(qt_ref, k_ref, vt_ref, lam_ref, sg_ref, o_ref, sa_ref, sb_ref, m_ref, l_ref, acc_ref,
                 *, tq, lambda_init):
    i = pl.program_id(2)
    qt = qt_ref[0, 0]
    feat = lax.broadcasted_iota(jnp.int32, qt.shape, 0)
    zero = jnp.zeros_like(qt)
    qst = jnp.concatenate([jnp.where(feat < HEAD, qt, zero), jnp.where(feat < HEAD, zero, qt)], axis=1)

    def scores(j, dst, diagonal=False):
        start = pl.multiple_of(j * tq, tq)
        s = _bdot(k_ref[0, pl.ds(start, tq), :], qst)
        if diagonal:
            key = lax.broadcasted_iota(jnp.int32, s.shape, 0)
            qry = lax.broadcasted_iota(jnp.int32, s.shape, 1)
            qry = jnp.where(qry >= tq, qry - tq, qry)
            s = jnp.where((key // CHUNK) <= (qry // CHUNK), s, NEG)
        dst[...] = s

    def absorb(src, j):
        m = m_ref[...]
        m_new = jnp.maximum(m, jnp.max(src[...], axis=0, keepdims=True))
        alpha = jnp.exp2(m - m_new)
        p = jnp.exp2(src[...] - m_new)
        m_ref[...] = m_new
        l_ref[...] = alpha * l_ref[...] + jnp.sum(p, axis=0, keepdims=True)
        acc_ref[...] = alpha * acc_ref[...] + _bdot(vt_ref[0, j], p.astype(BF16))

    m_ref[...] = jnp.full(m_ref.shape, NEG, F32)
    l_ref[...] = jnp.zeros(l_ref.shape, F32)
    acc_ref[...] = jnp.zeros(acc_ref.shape, F32)
    scores(i, sa_ref, diagonal=True)
    pairs = i // 2

    def pair(jj, _):
        scores(2 * jj, sb_ref)
        absorb(sa_ref, jnp.where(jj == 0, i, 2 * jj - 1))
        scores(2 * jj + 1, sa_ref)
        absorb(sb_ref, 2 * jj)
        return 0

    lax.fori_loop(0, pairs, pair, 0)
    held = jnp.where(pairs == 0, i, 2 * pairs - 1)

    @pl.when(i % 2 == 1)
    def _():
        scores(i - 1, sb_ref)
        absorb(sa_ref, held)
        absorb(sb_ref, i - 1)

    @pl.when(i % 2 == 0)
    def _():
        absorb(sa_ref, held)

    o = acc_ref[...] / l_ref[...]
    lv = lam_ref[...]
    lam = (jnp.exp(jnp.sum(lv[0:1] * lv[1:2], axis=-1, keepdims=True))
           - jnp.exp(jnp.sum(lv[2:3] * lv[3:4], axis=-1, keepdims=True)) + lambda_init)
    d = (o[:, :tq] - lam * o[:, tq:]).T
    o_ref[0] = (_rms(d, sg_ref[...], SUBLN_EPS) * (1.0 - lambda_init)).astype(BF16)


def _rope_tables(T):
    half = HEAD // 2
    inv = 1.0 / (ROPE_THETA ** (jnp.arange(0, HEAD, 2, dtype=F32) / HEAD))
    ang = jnp.arange(T, dtype=F32)[:, None] * inv[None, :]
    cos, sin = jnp.cos(ang), jnp.sin(ang)
    reps = LANES // half
    cos_t = jnp.tile(cos, (1, reps))
    sin_t = jnp.tile(jnp.concatenate([-sin, sin], axis=1), (1, reps // 2))
    return cos_t, sin_t


def _diff_attention(x, B, T, g, w_qkv, q_gain, k_gain, lam_vec, subln_g, lambda_init, *, tq=512):
    M, C = x.shape
    H = C // LANES
    nt = T // tq
    cos_t, sin_t = _rope_tables(T)
    reps = LANES // HEAD
    qg = (jnp.tile(q_gain, reps) * (HEAD ** -0.5 * math.log2(math.e))).reshape(1, LANES)
    kg = jnp.tile(k_gain, reps).reshape(1, LANES)
    row = pl.BlockSpec((tq, C), lambda i: (i, 0))
    table = pl.BlockSpec((tq, LANES), lambda i: (i % nt, 0))
    transposed = pl.BlockSpec((1, 1, C, tq), lambda i: (i // nt, i % nt, 0, 0))
    qt, k, vt = pl.pallas_call(
        _qkv_kernel,
        grid=(M // tq,),
        in_specs=[row, _resident((1, C)), _resident((C, 3 * C)), _resident((1, LANES)), _resident((1, LANES)),
                  table, table],
        out_specs=[transposed, row, transposed],
        out_shape=[jax.ShapeDtypeStruct((B, nt, C, tq), BF16), jax.ShapeDtypeStruct((M, C), BF16),
                   jax.ShapeDtypeStruct((B, nt, C, tq), BF16)],
        compiler_params=_params("parallel"),
        name="attn_qkv",
    )(x, g.reshape(1, C), w_qkv, qg, kg, cos_t, sin_t)
    o = pl.pallas_call(
        functools.partial(_attn_kernel, tq=tq, lambda_init=lambda_init),
        grid=(B, H, nt),
        in_specs=[pl.BlockSpec((1, 1, LANES, tq), lambda b, h, i: (b, i, h, 0)),
                  pl.BlockSpec((1, T, LANES), lambda b, h, i: (b, 0, h)),
                  pl.BlockSpec((1, nt, LANES, tq), lambda b, h, i: (b, 0, h, 0)),
                  _resident((4, HEAD)), _resident((1, LANES))],
        out_specs=pl.BlockSpec((1, tq, LANES), lambda b, h, i: (b, i, h)),
        out_shape=jax.ShapeDtypeStruct((B, T, C), BF16),
        scratch_shapes=[pltpu.VMEM((tq, 2 * tq), F32), pltpu.VMEM((tq, 2 * tq), F32),
                        pltpu.VMEM((1, 2 * tq), F32), pltpu.VMEM((1, 2 * tq), F32),
                        pltpu.VMEM((LANES, 2 * tq), F32)],
        compiler_params=_params("parallel", "parallel", "arbitrary"),
        name="attn_core",
    )(qt, k.reshape(B, T, C), vt, lam_vec, subln_g.reshape(1, LANES))
    return o.reshape(M, C)


def _rwkv_prep_kernel(x_ref, xp_ref, g_ref, mix_ref, wrkv_ref, w1_ref, w2_ref, a1_ref, a2_ref, g1_ref, g2_ref,
                      w0_ref, a0_ref, kk_ref, ka_ref,
                      r_out, lw_out, k_out, v_out, an_out, bn_out, g_out, *, tiles_per_seq):
    i = pl.program_id(0)
    tm, C = x_ref.shape
    g = g_ref[...]
    h = _rms(x_ref[...], g, NORM_EPS)
    h_last = _rms(xp_ref[SUBLANES - 1:SUBLANES, :], g, NORM_EPS)
    h_last = jnp.where(i % tiles_per_seq == 0, 0.0, h_last)
    row = lax.broadcasted_iota(jnp.int32, h.shape, 0)
    dx = jnp.where(row == 0, h_last, pltpu.roll(h, 1, 0)) - h
    mix = mix_ref[...]

    def mixed(n):
        return (h + dx * mix[n:n + 1]).astype(BF16)

    r = _bdot(mixed(0), wrkv_ref[:, 0:C])
    k = _bdot(mixed(2), wrkv_ref[:, C:2 * C])
    v = _bdot(mixed(3), wrkv_ref[:, 2 * C:])
    w_lora = _bdot(jnp.tanh(_bdot(mixed(1), w1_ref[...])).astype(BF16), w2_ref[...])
    a_lora = _bdot(_bdot(mixed(4), a1_ref[...]).astype(BF16), a2_ref[...])
    gate = _bdot(jax.nn.sigmoid(_bdot(mixed(5), g1_ref[...])).astype(BF16), g2_ref[...])

    z = -(w0_ref[...] + w_lora)
    softplus = jnp.maximum(z, 0.0) + jnp.log1p(jnp.exp(-jnp.abs(z)))
    lw = -jnp.exp(-softplus - 0.5)
    a = jax.nn.sigmoid(a0_ref[...] + a_lora)
    kk = k * kk_ref[...]
    k2 = k * (1.0 + (a - 1.0) * ka_ref[...])

    r_out[...] = r
    lw_out[...] = lw
    k_out[...] = k2
    v_out[...] = v
    g_out[...] = gate
    lane = lax.broadcasted_iota(jnp.int32, (tm, LANES), 1)
    first = lane < HEAD
    for s in range(C // LANES):
        sl = slice(s * LANES, (s + 1) * LANES)
        kks = kk[:, sl]
        kkn = kks * lax.rsqrt(jnp.maximum(_half_sums(kks * kks, first), 1e-24))
        an_out[:, sl] = -kkn
        bn_out[:, sl] = kkn * a[:, sl]


def _rwkv_scan_kernel(r_ref, lw_ref, k_ref, v_ref, a_ref, b_ref, g_ref, rk_ref, lg_ref, lb_ref, o_ref, h_ref,
                      *, n_chunks):
    L = CHUNK

    @pl.when(pl.program_id(2) == 0)
    def _():
        h_ref[...] = jnp.zeros_like(h_ref)

    row = lax.broadcasted_iota(jnp.int32, (2 * L, LANES), 0)
    col = lax.broadcasted_iota(jnp.int32, (2 * L, LANES), 1)
    block_diag = (row >= L) == (col >= L)
    rt = row & (L - 1)
    ct = col & (L - 1)
    strict = ct < rt
    incl = ct <= rt
    eye = (row == col).astype(F32)
    lane = lax.broadcasted_iota(jnp.int32, (L, LANES), 1)
    first = lane < HEAD
    tr = lax.broadcasted_iota(jnp.int32, (L, L), 0)
    tc = lax.broadcasted_iota(jnp.int32, (L, L), 1)
    tri = (tc <= tr).astype(BF16)

    def stack(z):
        return jnp.concatenate([jnp.where(first, z, 0.0), jnp.where(first, 0.0, z)], axis=0)

    def split3(z):
        hi = z.astype(BF16)
        r1 = z - hi.astype(F32)
        mid = r1.astype(BF16)
        return hi, mid, (r1 - mid.astype(F32)).astype(BF16)

    rk = rk_ref[...]
    lg = lg_ref[...]
    lb = lb_ref[...]

    chunks = range(n_chunks)
    sls = [pl.ds(c * L, L) for c in chunks]
    R = [r_ref[0, sl, :] for sl in sls]
    LW = [lw_ref[0, sl, :] for sl in sls]
    K = [k_ref[0, sl, :] for sl in sls]
    V = [v_ref[0, sl, :] for sl in sls]
    A = [a_ref[0, sl, :] for sl in sls]
    Bv = [b_ref[0, sl, :] for sl in sls]
    cs = [_bdot(tri, jnp.concatenate(split3(LW[c]), axis=1)) for c in chunks]
    cum = [z[:, :LANES] + z[:, LANES:2 * LANES] + z[:, 2 * LANES:] for z in cs]
    cum_last = [z[L - 1:L, :] for z in cum]
    inv = [jnp.exp(-z) for z in cum]
    to_end = [jnp.exp(cum_last[c] - cum[c]) for c in chunks]
    Rt = [R[c] * jnp.exp(cum[c]) for c in chunks]
    At = [A[c] * jnp.exp(cum[c] - LW[c]) for c in chunks]
    Kt = [K[c] * inv[c] for c in chunks]
    Bt = [Bv[c] * inv[c] for c in chunks]
    AtS = [stack(z) for z in At]
    RtS = [stack(z) for z in Rt]

    p0 = [_dot_nt(jnp.concatenate([AtS[c][:L], RtS[c][:L]], axis=0).astype(BF16),
                  jnp.concatenate([Bt[c], Kt[c]], axis=0).astype(BF16)) for c in chunks]
    p1 = [_dot_nt(jnp.concatenate([AtS[c][L:], RtS[c][L:]], axis=0).astype(BF16),
                  jnp.concatenate([Kt[c], Bt[c]], axis=0).astype(BF16)) for c in chunks]
    top = [jnp.concatenate([p0[c][:L], p1[c][:L]], axis=0) for c in chunks]
    bot = [jnp.concatenate([p0[c][L:], p1[c][L:]], axis=0) for c in chunks]
    n_ab = [jnp.where(block_diag & strict, z, 0.0) for z in top]
    m_ak = [jnp.where(block_diag | ~strict, 0.0, z) for z in top]
    m_rb = [jnp.where(block_diag & incl, z, 0.0) for z in bot]
    m_rk = [jnp.where(block_diag | ~incl, 0.0, z) for z in bot]

    tinv = [eye + z for z in n_ab]
    xb = [z.astype(BF16) for z in n_ab]
    xb = [_bdot(z, z).astype(BF16) for z in xb]
    for _ in range(4):
        xt = [_bdot(xb[c], jnp.concatenate([xb[c], tinv[c].astype(BF16)], axis=1)) for c in chunks]
        xb = [z[:, :LANES].astype(BF16) for z in xt]
        tinv = [tinv[c] + xt[c][:, LANES:] for c in chunks]
    tinv = [tinv[c] + _bdot(xb[c], tinv[c].astype(BF16)) for c in chunks]

    Vb = [z.astype(BF16) for z in V]
    zy = [_bdot(jnp.concatenate([m_ak[c], m_rk[c]], axis=0).astype(BF16),
                jnp.concatenate([Vb[c], Vb[c]], axis=0)) for c in chunks]
    z_ak = [jnp.where(block_diag, z[:2 * L], 0.0) for z in zy]
    y_rk = [jnp.where(block_diag, z[2 * L:], 0.0) for z in zy]
    wub = [_bdot(tinv[c].astype(BF16), jnp.concatenate([AtS[c], z_ak[c]], axis=1).astype(BF16)).astype(BF16)
           for c in chunks]
    qy = [_bdot(m_rb[c].astype(BF16), wub[c]) for c in chunks]
    q_eff = [(RtS[c] + qy[c][:, :LANES]).astype(BF16) for c in chunks]
    y_loc = [y_rk[c] + qy[c][:, LANES:] for c in chunks]
    mg = [_dot_tn(stack(Bv[c] * to_end[c]).astype(BF16), wub[c]) for c in chunks]
    m_c = [(eye * jnp.exp(cum_last[c]) + mg[c][:, :LANES]).astype(BF16) for c in chunks]
    g_c = [mg[c][:, LANES:] + _dot_tn(stack(K[c] * to_end[c]).astype(BF16), stack(V[c]).astype(BF16))
           for c in chunks]

    h = h_ref[...]
    ys = []
    for c in chunks:
        hb = h.astype(BF16)
        ys.append(_bdot(q_eff[c], hb) + y_loc[c])
        h = _bdot(m_c[c], hb) + g_c[c]
    h_ref[...] = h

    for c in chunks:
        y = ys[c][:L] + ys[c][L:]
        mu = _half_sums(y, first) * (1.0 / HEAD)
        d = y - mu
        var = _half_sums(d * d, first) * (1.0 / HEAD)
        yn = d * lax.rsqrt(var + RW_LNX_EPS) * lg + lb
        bonus = _half_sums(R[c] * K[c] * rk, first) * V[c]
        o_ref[0, sls[c], :] = ((yn + bonus) * g_ref[0, sls[c], :]).astype(BF16)


def _rwkv_time_mix(x, B, T, g, mix, w_rkv, w0, w1, w2, a0, a1, a2, g1, g2, k_k, k_a, r_k, lnx_g, lnx_b,
                   *, tm=256, tt=1024):
    M, C = x.shape
    row = pl.BlockSpec((tm, C), lambda i: (i, 0))
    prev = pl.BlockSpec((SUBLANES, C), lambda i: (jnp.maximum(i * (tm // SUBLANES) - 1, 0), 0))
    vec = lambda a: a.reshape(1, C)
    outs = pl.pallas_call(
        functools.partial(_rwkv_prep_kernel, tiles_per_seq=T // tm),
        grid=(M // tm,),
        in_specs=[row, prev, _resident((1, C)), _resident(mix.shape), _resident(w_rkv.shape),
                  _resident(w1.shape), _resident(w2.shape), _resident(a1.shape), _resident(a2.shape),
                  _resident(g1.shape), _resident(g2.shape)] + [_resident((1, C))] * 4,
        out_specs=[row] * 7,
        out_shape=[jax.ShapeDtypeStruct((M, C), F32)] * 7,
        compiler_params=_params("parallel"),
        name="rwkv_prep",
    )(x, x, vec(g), mix, w_rkv, w1, w2, a1, a2, g1, g2, vec(w0), vec(a0), vec(k_k), vec(k_a))
    seq = [z.reshape(B, T, C) for z in outs]
    slab = pl.BlockSpec((1, tt, LANES), lambda b, s, t: (b, t, s))
    pvec = pl.BlockSpec((1, LANES), lambda b, s, t: (0, s))
    y = pl.pallas_call(
        functools.partial(_rwkv_scan_kernel, n_chunks=tt // CHUNK),
        grid=(B, C // LANES, T // tt),
        in_specs=[slab] * 7 + [pvec] * 3,
        out_specs=slab,
        out_shape=jax.ShapeDtypeStruct((B, T, C), BF16),
        scratch_shapes=[pltpu.VMEM((LANES, LANES), F32)],
        compiler_params=_params("parallel", "parallel", "arbitrary"),
        name="rwkv_scan",
    )(*seq, vec(r_k), vec(lnx_g), vec(lnx_b))
    return y.reshape(M, C)


def _pool_kernel(x_ref, xh_ref, g_ref, o_ref):
    i = pl.program_id(1)
    tt, C = x_ref.shape[1], x_ref.shape[2]
    group = C // len(POOL_WINDOWS)
    g = g_ref[...]
    h = _rms(x_ref[0], g, NORM_EPS)
    halo = jnp.where(i == 0, 0.0, _rms(xh_ref[0], g, NORM_EPS))
    he = jnp.concatenate([halo, h], axis=0)
    t = i * tt + lax.broadcasted_iota(jnp.int32, (tt, 1), 0)
    for gi, win in enumerate(POOL_WINDOWS):
        sl = slice(gi * group, (gi + 1) * group)
        s = he[:, sl]
        shift = 1
        while shift < win:
            s = s + pltpu.roll(s, shift, 0)
            shift *= 2
        cnt = jnp.minimum(t + 1, win).astype(F32)
        o_ref[0, :, sl] = (s[POOL_HALO:] / cnt - h[:, sl]).astype(BF16)


def _multiscale_pool(x, B, T, g, *, tt=512):
    M, C = x.shape
    x3 = x.reshape(B, T, C)
    blocks = tt // POOL_HALO
    d = pl.pallas_call(
        _pool_kernel,
        grid=(B, T // tt),
        in_specs=[pl.BlockSpec((1, tt, C), lambda b, i: (b, i, 0)),
                  pl.BlockSpec((1, POOL_HALO, C), lambda b, i: (b, jnp.maximum(i * blocks - 1, 0), 0)),
                  _resident((1, C))],
        out_specs=pl.BlockSpec((1, tt, C), lambda b, i: (b, i, 0)),
        out_shape=jax.ShapeDtypeStruct((B, T, C), BF16),
        compiler_params=_params("parallel", "arbitrary"),
        name="pool",
    )(x3, x3, g.reshape(1, C))
    return d.reshape(M, C)


def kernel(x, mix_norm, ffn_norm, ffn_w_in, ffn_w_out, da_wqkv, da_wo, da_q_gain, da_k_gain, da_lambda, da_subln, rw_mix, rw_wrkv, rw_wo, rw_w0, rw_w1, rw_w2, rw_a0, rw_a1, rw_a2, rw_g1, rw_g2, rw_kk, rw_ka, rw_rk, rw_lnx_g, rw_lnx_b, pool_w, pool_scale):
    B, T, C = x.shape
    depth = mix_norm.shape[0]
    bf = lambda w: w.astype(BF16)
    xs = x.reshape(B * T, C)
    ia = ir = ip = 0
    for layer in range(depth):
        kind = layer % N_MIXERS
        scale = None
        if kind == 0:
            lambda_init = 0.8 - 0.6 * math.exp(-0.3 * layer)
            y = _diff_attention(xs, B, T, mix_norm[layer], bf(da_wqkv[ia]), da_q_gain[ia], da_k_gain[ia],
                                da_lambda[ia], da_subln[ia], lambda_init)
            w_mix = bf(da_wo[ia])
            ia += 1
        elif kind == 1:
            y = _rwkv_time_mix(xs, B, T, mix_norm[layer], rw_mix[ir], bf(rw_wrkv[ir]), rw_w0[ir], bf(rw_w1[ir]),
                               bf(rw_w2[ir]), rw_a0[ir], bf(rw_a1[ir]), bf(rw_a2[ir]), bf(rw_g1[ir]),
                               bf(rw_g2[ir]), rw_kk[ir], rw_ka[ir], rw_rk[ir], rw_lnx_g[ir], rw_lnx_b[ir])
            w_mix = bf(rw_wo[ir])
            ir += 1
        else:
            y = _multiscale_pool(xs, B, T, mix_norm[layer])
            w_mix = bf(jax.scipy.linalg.block_diag(*pool_w[ip]))
            scale = pool_scale[ip]
            ip += 1
        xs = _mix_ffn(xs, y, w_mix, scale, ffn_norm[layer], bf(ffn_w_in[layer]), bf(ffn_w_out[layer]))
    return xs.reshape(B, T, C)
```

```python
import functools
import math

import jax
import jax.numpy as jnp
from jax import lax
from jax.experimental import pallas as pl
from jax.experimental.pallas import tpu as pltpu

F32 = jnp.float32
BF16 = jnp.bfloat16

LANES = 128
SUBLANES = 8
VMEM_LIMIT_BYTES = 48 * 1024 * 1024

CHUNK = 64
HEAD = 64
N_MIXERS = 3
NORM_EPS = 1e-6
QK_EPS = 1e-6
SUBLN_EPS = 1e-5
ROPE_THETA = 10000.0
RW_LNX_EPS = 64e-5
POOL_WINDOWS = (2, 4, 8, 16)
POOL_HALO = 16
NEG = -1e30


def _rms(z, gain, eps):
    return z * lax.rsqrt(jnp.mean(z * z, axis=-1, keepdims=True) + eps) * gain


def _bdot(a, b):
    return jnp.dot(a, b, preferred_element_type=F32)


def _dot_nt(a, b):
    return lax.dot_general(a, b, (((1,), (1,)), ((), ())), preferred_element_type=F32)


def _dot_tn(a, b):
    return lax.dot_general(a, b, (((0,), (0,)), ((), ())), preferred_element_type=F32)


def _half_sums(z, first):
    lo = jnp.sum(jnp.where(first, z, 0.0), axis=-1, keepdims=True)
    hi = jnp.sum(jnp.where(first, 0.0, z), axis=-1, keepdims=True)
    return jnp.where(first, lo, hi)


def _params(*sem):
    return pltpu.CompilerParams(dimension_semantics=sem, vmem_limit_bytes=VMEM_LIMIT_BYTES)


def _resident(shape):
    return pl.BlockSpec(shape, lambda *_: (0,) * len(shape), pipeline_mode=pl.Buffered(1))


def _mix_ffn_kernel(*refs, hidden, tf, has_scale):
    if has_scale:
        x_ref, y_ref, wm_ref, sc_ref, g_ref, win_ref, wout_ref, o_ref = refs
    else:
        x_ref, y_ref, wm_ref, g_ref, win_ref, wout_ref, o_ref = refs
    y = _bdot(y_ref[...], wm_ref[...])
    if has_scale:
        y = y * sc_ref[...]
    x1 = x_ref[...] + y
    hn = _rms(x1, g_ref[...], NORM_EPS).astype(BF16)
    acc = x1
    for f in range(hidden // tf):
        gate = _bdot(hn, win_ref[:, f * tf:(f + 1) * tf])
        up = _bdot(hn, win_ref[:, hidden + f * tf:hidden + (f + 1) * tf])
        act = (gate * jax.nn.sigmoid(gate) * up).astype(BF16)
        acc = acc + _bdot(act, wout_ref[f * tf:(f + 1) * tf, :])
    o_ref[...] = acc


def _mix_ffn(x, y, w_mix, scale, g, w_in, w_out, *, tm=512, tf=256):
    M, C = x.shape
    hidden = w_out.shape[0]
    has_scale = scale is not None
    row = lambda w: pl.BlockSpec((tm, w), lambda i: (i, 0))
    in_specs = [row(C), row(C), _resident((C, C))]
    args = [x, y, w_mix]
    if has_scale:
        in_specs.append(_resident((1, C)))
        args.append(scale.reshape(1, C))
    in_specs += [_resident((1, C)), _resident((C, 2 * hidden)), _resident((hidden, C))]
    args += [g.reshape(1, C), w_in, w_out]
    return pl.pallas_call(
        functools.partial(_mix_ffn_kernel, hidden=hidden, tf=tf, has_scale=has_scale),
        grid=(M // tm,),
        in_specs=in_specs,
        out_specs=row(C),
        out_shape=jax.ShapeDtypeStruct((M, C), F32),
        compiler_params=_params("parallel"),
        name="mix_ffn",
    )(*args)


def _qkv_kernel(x_ref, g_ref, wqt_ref, wk_ref, wvt_ref, qgt_ref, kg_ref, cos_ref, sin_ref, cost_ref, sint_ref,
                qt_ref, k_ref, vt_ref):
    tm, C = x_ref.shape
    hn = _rms(x_ref[...], g_ref[...], NORM_EPS).astype(BF16)

    k = _bdot(hn, wk_ref[...])
    vt_ref[0, 0] = _dot_nt(wvt_ref[...], hn).astype(BF16)

    q3 = _dot_nt(wqt_ref[...], hn).reshape(C // HEAD, HEAD, tm)
    ms = jnp.mean(q3 * q3, axis=1, keepdims=True)
    qn = q3 * lax.rsqrt(ms + QK_EPS) * qgt_ref[...][None]
    partner = jnp.concatenate([qn[:, HEAD // 2:], qn[:, :HEAD // 2]], axis=1)
    q3 = qn * cost_ref[...][None] + partner * sint_ref[...][None]
    qt_ref[0, 0] = q3.reshape(C, tm).astype(BF16)

    cos = cos_ref[...]
    sin = sin_ref[...]
    lane = lax.broadcasted_iota(jnp.int32, cos.shape, 1)
    first = lane < HEAD
    low_half = (lane & (HEAD - 1)) < HEAD // 2
    for s in range(C // LANES):
        sl = slice(s * LANES, (s + 1) * LANES)
        z = k[:, sl]
        zn = z * lax.rsqrt(_half_sums(z * z, first) * (1.0 / HEAD) + QK_EPS) * kg_ref[...]
        partner = jnp.where(low_half, pltpu.roll(zn, LANES - HEAD // 2, 1), pltpu.roll(zn, HEAD // 2, 1))
        k_ref[:, sl] = (zn * cos + partner * sin).astype(BF16)


def _attn_kernel(qt_ref, k_ref, vt_ref, lam_ref, sg_ref, o_ref, s_refs, m_refs, l_refs, acc_refs,
                 *, tq, lambda_init):
    i = pl.program_id(2)
    feat = lax.broadcasted_iota(jnp.int32, (LANES, tq), 0)

    def stacked_q(h):
        qt = qt_ref[0, 0, h * LANES:(h + 1) * LANES, :]
        zero = jnp.zeros_like(qt)
        return jnp.concatenate([jnp.where(feat < HEAD, qt, zero), jnp.where(feat < HEAD, zero, qt)], axis=1)

    qst = [stacked_q(0), stacked_q(1)]

    def scores(h, j, diagonal=False):
        start = pl.multiple_of(j * tq, tq)
        s = _bdot(k_ref[0, pl.ds(start, tq), h * LANES:(h + 1) * LANES], qst[h])
        if diagonal:
            key = lax.broadcasted_iota(jnp.int32, s.shape, 0)
            qry = lax.broadcasted_iota(jnp.int32, s.shape, 1)
            qry = jnp.where(qry >= tq, qry - tq, qry)
            s = jnp.where((key // CHUNK) <= (qry // CHUNK), s, NEG)
        s_refs[h] = s

    def absorb(h, j):
        m = m_refs[h]
        m_new = jnp.maximum(m, jnp.max(s_refs[h], axis=0, keepdims=True))
        alpha = jnp.exp2(m - m_new)
        p = jnp.exp2(s_refs[h] - m_new)
        m_refs[h] = m_new
        l_refs[h] = alpha * l_refs[h] + jnp.sum(p, axis=0, keepdims=True)
        vt = vt_ref[0, j, h * LANES:(h + 1) * LANES, :]
        acc_refs[h] = alpha * acc_refs[h] + _bdot(vt, p.astype(BF16))

    def finish(h):
        o = acc_refs[h] / l_refs[h]
        lv = lam_ref[...]
        lam = (jnp.exp(jnp.sum(lv[0:1] * lv[1:2], axis=-1, keepdims=True))
               - jnp.exp(jnp.sum(lv[2:3] * lv[3:4], axis=-1, keepdims=True)) + lambda_init)
        d = (o[:, :tq] - lam * o[:, tq:]).T
        o_ref[0, :, h * LANES:(h + 1) * LANES] = (_rms(d, sg_ref[...], SUBLN_EPS)
                                                  * (1.0 - lambda_init)).astype(BF16)

    m_refs[...] = jnp.full(m_refs.shape, NEG, F32)
    l_refs[...] = jnp.zeros(l_refs.shape, F32)
    acc_refs[...] = jnp.zeros(acc_refs.shape, F32)
    scores(0, i, diagonal=True)
    scores(1, i, diagonal=True)
    absorb(0, i)

    def step(n, _):
        scores(0, n)
        absorb(1, jnp.where(n == 0, i, n - 1))
        scores(1, n)
        absorb(0, n)
        return 0

    lax.fori_loop(0, i, step, 0)
    absorb(1, jnp.where(i == 0, i, i - 1))
    finish(0)
    finish(1)


def _rope_tables(T):
    half = HEAD // 2
    inv = 1.0 / (ROPE_THETA ** (jnp.arange(0, HEAD, 2, dtype=F32) / HEAD))
    ang = jnp.arange(T, dtype=F32)[:, None] * inv[None, :]
    cos, sin = jnp.cos(ang), jnp.sin(ang)
    cos_h = jnp.concatenate([cos, cos], axis=1)
    sin_h = jnp.concatenate([-sin, sin], axis=1)
    reps = LANES // HEAD
    return jnp.tile(cos_h, (1, reps)), jnp.tile(sin_h, (1, reps)), cos_h.T, sin_h.T


def _diff_attention(x, B, T, g, w_qkv, q_gain, k_gain, lam_vec, subln_g, lambda_init, *, tq=512):
    M, C = x.shape
    H = C // LANES
    nt = T // tq
    cos_t, sin_t, cos_tt, sin_tt = _rope_tables(T)
    qgt = jnp.broadcast_to((q_gain * (HEAD ** -0.5 * math.log2(math.e)))[:, None], (HEAD, tq))
    kg = jnp.tile(k_gain, LANES // HEAD).reshape(1, LANES)
    wqt, wk, wvt = w_qkv[:, :C].T, w_qkv[:, C:2 * C], w_qkv[:, 2 * C:].T
    row = pl.BlockSpec((tq, C), lambda i: (i, 0))
    table = pl.BlockSpec((tq, LANES), lambda i: (i % nt, 0))
    table_t = pl.BlockSpec((HEAD, tq), lambda i: (0, i % nt))
    transposed = pl.BlockSpec((1, 1, C, tq), lambda i: (i // nt, i % nt, 0, 0))
    qt, k, vt = pl.pallas_call(
        _qkv_kernel,
        grid=(M // tq,),
        in_specs=[row, _resident((1, C)), _resident((C, C)), _resident((C, C)), _resident((C, C)),
                  _resident((HEAD, tq)), _resident((1, LANES)), table, table, table_t, table_t],
        out_specs=[transposed, row, transposed],
        out_shape=[jax.ShapeDtypeStruct((B, nt, C, tq), BF16), jax.ShapeDtypeStruct((M, C), BF16),
                   jax.ShapeDtypeStruct((B, nt, C, tq), BF16)],
        compiler_params=_params("parallel"),
        name="attn_qkv",
    )(x, g.reshape(1, C), wqt, wk, wvt, qgt, kg, cos_t, sin_t, cos_tt, sin_tt)
    o = pl.pallas_call(
        functools.partial(_attn_kernel, tq=tq, lambda_init=lambda_init),
        grid=(B, H // 2, nt),
        in_specs=[pl.BlockSpec((1, 1, 2 * LANES, tq), lambda b, h, i: (b, i, h, 0)),
                  pl.BlockSpec((1, T, 2 * LANES), lambda b, h, i: (b, 0, h)),
                  pl.BlockSpec((1, nt, 2 * LANES, tq), lambda b, h, i: (b, 0, h, 0)),
                  _resident((4, HEAD)), _resident((1, LANES))],
        out_specs=pl.BlockSpec((1, tq, 2 * LANES), lambda b, h, i: (b, i, h)),
        out_shape=jax.ShapeDtypeStruct((B, T, C), BF16),
        scratch_shapes=[pltpu.VMEM((2, tq, 2 * tq), F32), pltpu.VMEM((2, 1, 2 * tq), F32),
                        pltpu.VMEM((2, 1, 2 * tq), F32), pltpu.VMEM((2, LANES, 2 * tq), F32)],
        compiler_params=_params("parallel", "parallel", "arbitrary"),
        name="attn_core",
    )(qt, k.reshape(B, T, C), vt, lam_vec, subln_g.reshape(1, LANES))
    return o.reshape(M, C)


def _rwkv_prep_kernel(x_ref, xp_ref, g_ref, mix_ref, wrkv_ref, w1_ref, w2_ref, a1_ref, a2_ref, g1_ref, g2_ref,
                      w0_ref, a0_ref, kk_ref, ka_ref,
                      r_out, lw_out, k_out, v_out, an_out, bn_out, g_out, *, tiles_per_seq):
    i = pl.program_id(0)
    tm, C = x_ref.shape
    g = g_ref[...]
    h = _rms(x_ref[...], g, NORM_EPS)
    h_last = _rms(xp_ref[SUBLANES - 1:SUBLANES, :], g, NORM_EPS)
    h_last = jnp.where(i % tiles_per_seq == 0, 0.0, h_last)
    row = lax.broadcasted_iota(jnp.int32, h.shape, 0)
    dx = jnp.where(row == 0, h_last, pltpu.roll(h, 1, 0)) - h
    mix = mix_ref[...]

    def mixed(n):
        return (h + dx * mix[n:n + 1]).astype(BF16)

    a_lora = _bdot(_bdot(mixed(4), a1_ref[...]).astype(BF16), a2_ref[...])
    k = _bdot(mixed(2), wrkv_ref[:, C:2 * C])
    w_lora = _bdot(jnp.tanh(_bdot(mixed(1), w1_ref[...])).astype(BF16), w2_ref[...])
    g_out[...] = _bdot(jax.nn.sigmoid(_bdot(mixed(5), g1_ref[...])).astype(BF16), g2_ref[...])
    r_out[...] = _bdot(mixed(0), wrkv_ref[:, 0:C])
    v_out[...] = _bdot(mixed(3), wrkv_ref[:, 2 * C:])

    z = -(w0_ref[...] + w_lora)
    softplus = jnp.maximum(z, 0.0) + jnp.log1p(jnp.exp(-jnp.abs(z)))
    lw_out[...] = -jnp.exp(-softplus - 0.5)
    a = jax.nn.sigmoid(a0_ref[...] + a_lora)
    kk = k * kk_ref[...]
    k_out[...] = k * (1.0 + (a - 1.0) * ka_ref[...])

    lane = lax.broadcasted_iota(jnp.int32, (tm, LANES), 1)
    first = lane < HEAD
    for s in range(C // LANES):
        sl = slice(s * LANES, (s + 1) * LANES)
        kks = kk[:, sl]
        kkn = kks * lax.rsqrt(jnp.maximum(_half_sums(kks * kks, first), 1e-24))
        an_out[:, sl] = -kkn
        bn_out[:, sl] = kkn * a[:, sl]


def _rwkv_scan_kernel(r_ref, lw_ref, k_ref, v_ref, a_ref, b_ref, g_ref, rk_ref, lg_ref, lb_ref, o_ref, h_ref,
                      *, n_chunks, group):
    @pl.when(pl.program_id(2) == 0)
    def _():
        h_ref[...] = jnp.zeros_like(h_ref)

    for base in range(0, n_chunks, group):
        _rwkv_scan_group(r_ref, lw_ref, k_ref, v_ref, a_ref, b_ref, g_ref, rk_ref, lg_ref, lb_ref, o_ref, h_ref,
                         base=base, n_chunks=group)


def _rwkv_scan_group(r_ref, lw_ref, k_ref, v_ref, a_ref, b_ref, g_ref, rk_ref, lg_ref, lb_ref, o_ref, h_ref,
                     *, base, n_chunks):
    L = CHUNK

    row = lax.broadcasted_iota(jnp.int32, (2 * L, LANES), 0)
    col = lax.broadcasted_iota(jnp.int32, (2 * L, LANES), 1)
    block_diag = (row >= L) == (col >= L)
    rt = row & (L - 1)
    ct = col & (L - 1)
    strict = ct < rt
    incl = ct <= rt
    eye = (row == col).astype(F32)
    lane = lax.broadcasted_iota(jnp.int32, (L, LANES), 1)
    first = lane < HEAD
    tr = lax.broadcasted_iota(jnp.int32, (L, L), 0)
    tc = lax.broadcasted_iota(jnp.int32, (L, L), 1)
    tri = (tc <= tr).astype(BF16)

    def stack(z):
        return jnp.concatenate([jnp.where(first, z, 0.0), jnp.where(first, 0.0, z)], axis=0)

    def split3(z):
        hi = z.astype(BF16)
        r1 = z - hi.astype(F32)
        mid = r1.astype(BF16)
        return hi, mid, (r1 - mid.astype(F32)).astype(BF16)

    rk = rk_ref[...]
    lg = lg_ref[...]
    lb = lb_ref[...]

    chunks = range(n_chunks)
    sls = [pl.ds((base + c) * L, L) for c in chunks]
    R = [r_ref[0, sl, :] for sl in sls]
    LW = [lw_ref[0, sl, :] for sl in sls]
    K = [k_ref[0, sl, :] for sl in sls]
    V = [v_ref[0, sl, :] for sl in sls]
    A = [a_ref[0, sl, :] for sl in sls]
    Bv = [b_ref[0, sl, :] for sl in sls]
    cs = [_bdot(tri, jnp.concatenate(split3(LW[c]), axis=1)) for c in chunks]
    cum = [z[:, :LANES] + z[:, LANES:2 * LANES] + z[:, 2 * LANES:] for z in cs]
    cum_last = [z[L - 1:L, :] for z in cum]
    inv = [jnp.exp(-z) for z in cum]
    to_end = [jnp.exp(cum_last[c] - cum[c]) for c in chunks]
    Rt = [R[c] * jnp.exp(cum[c]) for c in chunks]
    At = [A[c] * jnp.exp(cum[c] - LW[c]) for c in chunks]
    Kt = [K[c] * inv[c] for c in chunks]
    Bt = [Bv[c] * inv[c] for c in chunks]
    AtS = [stack(z) for z in At]
    RtS = [stack(z) for z in Rt]

    p0 = [_dot_nt(jnp.concatenate([AtS[c][:L], RtS[c][:L]], axis=0).astype(BF16),
                  jnp.concatenate([Bt[c], Kt[c]], axis=0).astype(BF16)) for c in chunks]
    p1 = [_dot_nt(jnp.concatenate([AtS[c][L:], RtS[c][L:]], axis=0).astype(BF16),
                  jnp.concatenate([Kt[c], Bt[c]], axis=0).astype(BF16)) for c in chunks]
    top = [jnp.concatenate([p0[c][:L], p1[c][:L]], axis=0) for c in chunks]
    bot = [jnp.concatenate([p0[c][L:], p1[c][L:]], axis=0) for c in chunks]
    n_ab = [jnp.where(block_diag & strict, z, 0.0) for z in top]
    m_ak = [jnp.where(block_diag | ~strict, 0.0, z) for z in top]
    m_rb = [jnp.where(block_diag & incl, z, 0.0) for z in bot]
    m_rk = [jnp.where(block_diag | ~incl, 0.0, z) for z in bot]

    tinv = [eye + z for z in n_ab]
    xb = [z.astype(BF16) for z in n_ab]
    xb = [_bdot(z, z).astype(BF16) for z in xb]
    for _ in range(4):
        xt = [_bdot(xb[c], jnp.concatenate([xb[c], tinv[c].astype(BF16)], axis=1)) for c in chunks]
        xb = [z[:, :LANES].astype(BF16) for z in xt]
        tinv = [tinv[c] + xt[c][:, LANES:] for c in chunks]
    tinv = [tinv[c] + _bdot(xb[c], tinv[c].astype(BF16)) for c in chunks]

    Vb = [z.astype(BF16) for z in V]
    zy = [_bdot(jnp.concatenate([m_ak[c], m_rk[c]], axis=0).astype(BF16),
                jnp.concatenate([Vb[c], Vb[c]], axis=0)) for c in chunks]
    z_ak = [jnp.where(block_diag, z[:2 * L], 0.0) for z in zy]
    y_rk = [jnp.where(block_diag, z[2 * L:], 0.0) for z in zy]
    wub = [_bdot(tinv[c].astype(BF16), jnp.concatenate([AtS[c], z_ak[c]], axis=1).astype(BF16)).astype(BF16)
           for c in chunks]
    qy = [_bdot(m_rb[c].astype(BF16), wub[c]) for c in chunks]
    q_eff = [(RtS[c] + qy[c][:, :LANES]).astype(BF16) for c in chunks]
    y_loc = [y_rk[c] + qy[c][:, LANES:] for c in chunks]
    mg = [_dot_tn(stack(Bv[c] * to_end[c]).astype(BF16), wub[c]) for c in chunks]
    m_c = [(eye * jnp.exp(cum_last[c]) + mg[c][:, :LANES]).astype(BF16) for c in chunks]
    g_c = [mg[c][:, LANES:] + _dot_tn(stack(K[c] * to_end[c]).astype(BF16), stack(V[c]).astype(BF16))
           for c in chunks]

    h = h_ref[...]
    ys = []
    for c in chunks:
        hb = h.astype(BF16)
        ys.append(_bdot(q_eff[c], hb) + y_loc[c])
        h = _bdot(m_c[c], hb) + g_c[c]
    h_ref[...] = h

    for c in chunks:
        y = ys[c][:L] + ys[c][L:]
        mu = _half_sums(y, first) * (1.0 / HEAD)
        d = y - mu
        var = _half_sums(d * d, first) * (1.0 / HEAD)
        yn = d * lax.rsqrt(var + RW_LNX_EPS) * lg + lb
        bonus = _half_sums(R[c] * K[c] * rk, first) * V[c]
        o_ref[0, sls[c], :] = ((yn + bonus) * g_ref[0, sls[c], :]).astype(BF16)


def _rwkv_time_mix(x, B, T, g, mix, w_rkv, w0, w1, w2, a0, a1, a2, g1, g2, k_k, k_a, r_k, lnx_g, lnx_b,
                   *, tm=512, tt=1024, group=16):
    M, C = x.shape
    row = pl.BlockSpec((tm, C), lambda i: (i, 0))
    prev = pl.BlockSpec((SUBLANES, C), lambda i: (jnp.maximum(i * (tm // SUBLANES) - 1, 0), 0))
    vec = lambda a: a.reshape(1, C)
    outs = pl.pallas_call(
        functools.partial(_rwkv_prep_kernel, tiles_per_seq=T // tm),
        grid=(M // tm,),
        in_specs=[row, prev, _resident((1, C)), _resident(mix.shape), _resident(w_rkv.shape),
                  _resident(w1.shape), _resident(w2.shape), _resident(a1.shape), _resident(a2.shape),
                  _resident(g1.shape), _resident(g2.shape)] + [_resident((1, C))] * 4,
        out_specs=[row] * 7,
        out_shape=[jax.ShapeDtypeStruct((M, C), F32)] * 7,
        compiler_params=_params("parallel"),
        name="rwkv_prep",
    )(x, x, vec(g), mix, w_rkv, w1, w2, a1, a2, g1, g2, vec(w0), vec(a0), vec(k_k), vec(k_a))
    seq = [z.reshape(B, T, C) for z in outs]
    slab = pl.BlockSpec((1, tt, LANES), lambda b, s, t: (b, t, s))
    pvec = pl.BlockSpec((1, LANES), lambda b, s, t: (0, s))
    y = pl.pallas_call(
        functools.partial(_rwkv_scan_kernel, n_chunks=tt // CHUNK, group=group),
        grid=(B, C // LANES, T // tt),
        in_specs=[slab] * 7 + [pvec] * 3,
        out_specs=slab,
        out_shape=jax.ShapeDtypeStruct((B, T, C), BF16),
        scratch_shapes=[pltpu.VMEM((LANES, LANES), F32)],
        compiler_params=_params("parallel", "parallel", "arbitrary"),
        name="rwkv_scan",
    )(*seq, vec(r_k), vec(lnx_g), vec(lnx_b))
    return y.reshape(M, C)


def _pool_kernel(x_ref, xh_ref, g_ref, o_ref):
    i = pl.program_id(1)
    tt, C = x_ref.shape[1], x_ref.shape[2]
    group = C // len(POOL_WINDOWS)
    g = g_ref[...]
    h = _rms(x_ref[0], g, NORM_EPS)
    halo = jnp.where(i == 0, 0.0, _rms(xh_ref[0], g, NORM_EPS))
    he = jnp.concatenate([halo, h], axis=0)
    t = i * tt + lax.broadcasted_iota(jnp.int32, (tt, 1), 0)
    for gi, win in enumerate(POOL_WINDOWS):
        sl = slice(gi * group, (gi + 1) * group)
        s = he[:, sl]
        shift = 1
        while shift < win:
            s = s + pltpu.roll(s, shift, 0)
            shift *= 2
        cnt = jnp.minimum(t + 1, win).astype(F32)
        o_ref[0, :, sl] = (s[POOL_HALO:] / cnt - h[:, sl]).astype(BF16)


def _multiscale_pool(x, B, T, g, *, tt=512):
    M, C = x.shape
    x3 = x.reshape(B, T, C)
    blocks = tt // POOL_HALO
    d = pl.pallas_call(
        _pool_kernel,
        grid=(B, T // tt),
        in_specs=[pl.BlockSpec((1, tt, C), lambda b, i: (b, i, 0)),
                  pl.BlockSpec((1, POOL_HALO, C), lambda b, i: (b, jnp.maximum(i * blocks - 1, 0), 0)),
                  _resident((1, C))],
        out_specs=pl.BlockSpec((1, tt, C), lambda b, i: (b, i, 0)),
        out_shape=jax.ShapeDtypeStruct((B, T, C), BF16),
        compiler_params=_params("parallel", "arbitrary"),
        name="pool",
    )(x3, x3, g.reshape(1, C))
    return d.reshape(M, C)


def kernel(x, mix_norm, ffn_norm, ffn_w_in, ffn_w_out, da_wqkv, da_wo, da_q_gain, da_k_gain, da_lambda, da_subln, rw_mix, rw_wrkv, rw_wo, rw_w0, rw_w1, rw_w2, rw_a0, rw_a1, rw_a2, rw_g1, rw_g2, rw_kk, rw_ka, rw_rk, rw_lnx_g, rw_lnx_b, pool_w, pool_scale):
    B, T, C = x.shape
    depth = mix_norm.shape[0]
    bf = lambda w: w.astype(BF16)
    xs = x.reshape(B * T, C)
    ia = ir = ip = 0
    for layer in range(depth):
        kind = layer % N_MIXERS
        scale = None
        if kind == 0:
            lambda_init = 0.8 - 0.6 * math.exp(-0.3 * layer)
            y = _diff_attention(xs, B, T, mix_norm[layer], bf(da_wqkv[ia]), da_q_gain[ia], da_k_gain[ia],
                                da_lambda[ia], da_subln[ia], lambda_init)
            w_mix = bf(da_wo[ia])
            ia += 1
        elif kind == 1:
            y = _rwkv_time_mix(xs, B, T, mix_norm[layer], rw_mix[ir], bf(rw_wrkv[ir]), rw_w0[ir], bf(rw_w1[ir]),
                               bf(rw_w2[ir]), rw_a0[ir], bf(rw_a1[ir]), bf(rw_a2[ir]), bf(rw_g1[ir]),
                               bf(rw_g2[ir]), rw_kk[ir], rw_ka[ir], rw_rk[ir], rw_lnx_g[ir], rw_lnx_b[ir])
            w_mix = bf(rw_wo[ir])
            ir += 1
        else:
            y = _multiscale_pool(xs, B, T, mix_norm[layer])
            w_mix = bf(jax.scipy.linalg.block_diag(*pool_w[ip]))
            scale = pool_scale[ip]
            ip += 1
        xs = _mix_ffn(xs, y, w_mix, scale, ffn_norm[layer], bf(ffn_w_in[layer]), bf(ffn_w_out[layer]))
    return xs.reshape(B, T, C)
```

```python
import functools
import math

import jax
import jax.numpy as jnp
from jax import lax
from jax.experimental import pallas as pl
from jax.experimental.pallas import tpu as pltpu

F32 = jnp.float32
BF16 = jnp.bfloat16

LANES = 128
SUBLANES = 8
VMEM_LIMIT_BYTES = 48 * 1024 * 1024

CHUNK = 64
HEAD = 64
N_MIXERS = 3
NORM_EPS = 1e-6
QK_EPS = 1e-6
SUBLN_EPS = 1e-5
ROPE_THETA = 10000.0
RW_LNX_EPS = 64e-5
POOL_WINDOWS = (2, 4, 8, 16)
POOL_HALO = 16
NEG = -1e30


def _rms(z, gain, eps):
    return z * lax.rsqrt(jnp.mean(z * z, axis=-1, keepdims=True) + eps) * gain


def _bdot(a, b):
    return jnp.dot(a, b, preferred_element_type=F32)


def _dot_nt(a, b):
    return lax.dot_general(a, b, (((1,), (1,)), ((), ())), preferred_element_type=F32)


def _dot_tn(a, b):
    return lax.dot_general(a, b, (((0,), (0,)), ((), ())), preferred_element_type=F32)


def _half_sums(z, first):
    lo = jnp.sum(jnp.where(first, z, 0.0), axis=-1, keepdims=True)
    hi = jnp.sum(jnp.where(first, 0.0, z), axis=-1, keepdims=True)
    return jnp.where(first, lo, hi)


def _params(*sem):
    return pltpu.CompilerParams(dimension_semantics=sem, vmem_limit_bytes=VMEM_LIMIT_BYTES)


def _resident(shape):
    return pl.BlockSpec(shape, lambda *_: (0,) * len(shape), pipeline_mode=pl.Buffered(1))


def _ffn_residual(x1, g_ref, win_ref, wout_ref, o_ref, hidden, tf):
    hn = _rms(x1, g_ref[...], NORM_EPS).astype(BF16)
    acc = x1
    for f in range(hidden // tf):
        gate = _bdot(hn, win_ref[:, f * tf:(f + 1) * tf])
        up = _bdot(hn, win_ref[:, hidden + f * tf:hidden + (f + 1) * tf])
        act = (gate * jax.nn.sigmoid(gate) * up).astype(BF16)
        acc = acc + _bdot(act, wout_ref[f * tf:(f + 1) * tf, :])
    o_ref[...] = acc


def _mix_ffn_kernel(x_ref, y_ref, wm_ref, g_ref, win_ref, wout_ref, o_ref, *, hidden, tf):
    x1 = x_ref[...] + _bdot(y_ref[...], wm_ref[...])
    _ffn_residual(x1, g_ref, win_ref, wout_ref, o_ref, hidden, tf)


def _mix_ffn(x, y, w_mix, g, w_in, w_out, *, tm=512, tf=256):
    M, C = x.shape
    hidden = w_out.shape[0]
    row = pl.BlockSpec((tm, C), lambda i: (i, 0))
    return pl.pallas_call(
        functools.partial(_mix_ffn_kernel, hidden=hidden, tf=tf),
        grid=(M // tm,),
        in_specs=[row, row, _resident((C, C)), _resident((1, C)), _resident((C, 2 * hidden)),
                  _resident((hidden, C))],
        out_specs=row,
        out_shape=jax.ShapeDtypeStruct((M, C), F32),
        compiler_params=_params("parallel"),
        name="mix_ffn",
    )(x, y, w_mix, g.reshape(1, C), w_in, w_out)


def _qkv_kernel(x_ref, g_ref, wqt_ref, wk_ref, wvt_ref, qgt_ref, kg_ref, cos_ref, sin_ref, cost_ref, sint_ref,
                qt_ref, k_ref, vt_ref):
    tm, C = x_ref.shape
    hn = _rms(x_ref[...], g_ref[...], NORM_EPS).astype(BF16)

    k = _bdot(hn, wk_ref[...])
    vt_ref[0, 0] = _dot_nt(wvt_ref[...], hn).astype(BF16)

    q3 = _dot_nt(wqt_ref[...], hn).reshape(C // HEAD, HEAD, tm)
    ms = jnp.mean(q3 * q3, axis=1, keepdims=True)
    qn = q3 * lax.rsqrt(ms + QK_EPS) * qgt_ref[...][None]
    partner = jnp.concatenate([qn[:, HEAD // 2:], qn[:, :HEAD // 2]], axis=1)
    q3 = qn * cost_ref[...][None] + partner * sint_ref[...][None]
    qt_ref[0, 0] = q3.reshape(C, tm).astype(BF16)

    cos = cos_ref[...]
    sin = sin_ref[...]
    lane = lax.broadcasted_iota(jnp.int32, cos.shape, 1)
    first = lane < HEAD
    low_half = (lane & (HEAD - 1)) < HEAD // 2
    for s in range(C // LANES):
        sl = slice(s * LANES, (s + 1) * LANES)
        z = k[:, sl]
        zn = z * lax.rsqrt(_half_sums(z * z, first) * (1.0 / HEAD) + QK_EPS) * kg_ref[...]
        partner = jnp.where(low_half, pltpu.roll(zn, LANES - HEAD // 2, 1), pltpu.roll(zn, HEAD // 2, 1))
        k_ref[:, sl] = (zn * cos + partner * sin).astype(BF16)


def _attn_kernel(qt_ref, k_ref, vt_ref, lam_ref, sg_ref, o_ref, s_refs, m_refs, l_refs, acc_refs,
                 *, tq, lambda_init):
    i = pl.program_id(2)
    feat = lax.broadcasted_iota(jnp.int32, (LANES, tq), 0)

    def stacked_q(h):
        qt = qt_ref[0, 0, h * LANES:(h + 1) * LANES, :]
        zero = jnp.zeros_like(qt)
        return jnp.concatenate([jnp.where(feat < HEAD, qt, zero), jnp.where(feat < HEAD, zero, qt)], axis=1)

    qst = [stacked_q(0), stacked_q(1)]

    def scores(h, j, diagonal=False):
        start = pl.multiple_of(j * tq, tq)
        s = _bdot(k_ref[0, pl.ds(start, tq), h * LANES:(h + 1) * LANES], qst[h])
        if diagonal:
            key = lax.broadcasted_iota(jnp.int32, s.shape, 0)
            qry = lax.broadcasted_iota(jnp.int32, s.shape, 1)
            qry = jnp.where(qry >= tq, qry - tq, qry)
            s = jnp.where((key // CHUNK) <= (qry // CHUNK), s, NEG)
        s_refs[h] = s

    def absorb(h, j):
        m = m_refs[h]
        m_new = jnp.maximum(m, jnp.max(s_refs[h], axis=0, keepdims=True))
        alpha = jnp.exp2(m - m_new)
        p = jnp.exp2(s_refs[h] - m_new)
        m_refs[h] = m_new
        l_refs[h] = alpha * l_refs[h] + jnp.sum(p, axis=0, keepdims=True)
        vt = vt_ref[0, j, h * LANES:(h + 1) * LANES, :]
        acc_refs[h] = alpha * acc_refs[h] + _bdot(vt, p.astype(BF16))

    def finish(h):
        o = acc_refs[h] / l_refs[h]
        lv = lam_ref[...]
        lam = (jnp.exp(jnp.sum(lv[0:1] * lv[1:2], axis=-1, keepdims=True))
               - jnp.exp(jnp.sum(lv[2:3] * lv[3:4], axis=-1, keepdims=True)) + lambda_init)
        d = (o[:, :tq] - lam * o[:, tq:]).T
        o_ref[0, :, h * LANES:(h + 1) * LANES] = (_rms(d, sg_ref[...], SUBLN_EPS)
                                                  * (1.0 - lambda_init)).astype(BF16)

    m_refs[...] = jnp.full(m_refs.shape, NEG, F32)
    l_refs[...] = jnp.zeros(l_refs.shape, F32)
    acc_refs[...] = jnp.zeros(acc_refs.shape, F32)
    scores(0, i, diagonal=True)
    scores(1, i, diagonal=True)
    absorb(0, i)

    def step(n, _):
        scores(0, n)
        absorb(1, jnp.where(n == 0, i, n - 1))
        scores(1, n)
        absorb(0, n)
        return 0

    lax.fori_loop(0, i, step, 0)
    absorb(1, jnp.where(i == 0, i, i - 1))
    finish(0)
    finish(1)


def _rope_tables(T):
    half = HEAD // 2
    inv = 1.0 / (ROPE_THETA ** (jnp.arange(0, HEAD, 2, dtype=F32) / HEAD))
    ang = jnp.arange(T, dtype=F32)[:, None] * inv[None, :]
    cos, sin = jnp.cos(ang), jnp.sin(ang)
    cos_h = jnp.concatenate([cos, cos], axis=1)
    sin_h = jnp.concatenate([-sin, sin], axis=1)
    reps = LANES // HEAD
    return jnp.tile(cos_h, (1, reps)), jnp.tile(sin_h, (1, reps)), cos_h.T, sin_h.T


def _diff_attention(x, B, T, g, w_qkv, q_gain, k_gain, lam_vec, subln_g, lambda_init, *, tq=512):
    M, C = x.shape
    H = C // LANES
    nt = T // tq
    cos_t, sin_t, cos_tt, sin_tt = _rope_tables(T)
    qgt = jnp.broadcast_to((q_gain * (HEAD ** -0.5 * math.log2(math.e)))[:, None], (HEAD, tq))
    kg = jnp.tile(k_gain, LANES // HEAD).reshape(1, LANES)
    wqt, wk, wvt = w_qkv[:, :C].T, w_qkv[:, C:2 * C], w_qkv[:, 2 * C:].T
    row = pl.BlockSpec((tq, C), lambda i: (i, 0))
    table = pl.BlockSpec((tq, LANES), lambda i: (i % nt, 0))
    table_t = pl.BlockSpec((HEAD, tq), lambda i: (0, i % nt))
    transposed = pl.BlockSpec((1, 1, C, tq), lambda i: (i // nt, i % nt, 0, 0))
    qt, k, vt = pl.pallas_call(
        _qkv_kernel,
        grid=(M // tq,),
        in_specs=[row, _resident((1, C)), _resident((C, C)), _resident((C, C)), _resident((C, C)),
                  _resident((HEAD, tq)), _resident((1, LANES)), table, table, table_t, table_t],
        out_specs=[transposed, row, transposed],
        out_shape=[jax.ShapeDtypeStruct((B, nt, C, tq), BF16), jax.ShapeDtypeStruct((M, C), BF16),
                   jax.ShapeDtypeStruct((B, nt, C, tq), BF16)],
        compiler_params=_params("parallel"),
        name="attn_qkv",
    )(x, g.reshape(1, C), wqt, wk, wvt, qgt, kg, cos_t, sin_t, cos_tt, sin_tt)
    o = pl.pallas_call(
        functools.partial(_attn_kernel, tq=tq, lambda_init=lambda_init),
        grid=(B, H // 2, nt),
        in_specs=[pl.BlockSpec((1, 1, 2 * LANES, tq), lambda b, h, i: (b, i, h, 0)),
                  pl.BlockSpec((1, T, 2 * LANES), lambda b, h, i: (b, 0, h)),
                  pl.BlockSpec((1, nt, 2 * LANES, tq), lambda b, h, i: (b, 0, h, 0)),
                  _resident((4, HEAD)), _resident((1, LANES))],
        out_specs=pl.BlockSpec((1, tq, 2 * LANES), lambda b, h, i: (b, i, h)),
        out_shape=jax.ShapeDtypeStruct((B, T, C), BF16),
        scratch_shapes=[pltpu.VMEM((2, tq, 2 * tq), F32), pltpu.VMEM((2, 1, 2 * tq), F32),
                        pltpu.VMEM((2, 1, 2 * tq), F32), pltpu.VMEM((2, LANES, 2 * tq), F32)],
        compiler_params=_params("parallel", "parallel", "arbitrary"),
        name="attn_core",
    )(qt, k.reshape(B, T, C), vt, lam_vec, subln_g.reshape(1, LANES))
    return o.reshape(M, C)


def _rwkv_prep_kernel(x_ref, xp_ref, g_ref, mix_ref, wrkv_ref, w1_ref, w2_ref, a1_ref, a2_ref, g1_ref, g2_ref,
                      w0_ref, a0_ref, kk_ref, ka_ref,
                      r_out, lw_out, k_out, v_out, an_out, bn_out, g_out, *, tiles_per_seq):
    i = pl.program_id(0)
    tm, C = x_ref.shape
    g = g_ref[...]
    h = _rms(x_ref[...], g, NORM_EPS)
    h_last = _rms(xp_ref[SUBLANES - 1:SUBLANES, :], g, NORM_EPS)
    h_last = jnp.where(i % tiles_per_seq == 0, 0.0, h_last)
    row = lax.broadcasted_iota(jnp.int32, h.shape, 0)
    dx = jnp.where(row == 0, h_last, pltpu.roll(h, 1, 0)) - h
    mix = mix_ref[...]

    def mixed(n):
        return (h + dx * mix[n:n + 1]).astype(BF16)

    a_lora = _bdot(_bdot(mixed(4), a1_ref[...]).astype(BF16), a2_ref[...])
    k = _bdot(mixed(2), wrkv_ref[:, C:2 * C])
    w_lora = _bdot(jnp.tanh(_bdot(mixed(1), w1_ref[...])).astype(BF16), w2_ref[...])
    g_out[...] = _bdot(jax.nn.sigmoid(_bdot(mixed(5), g1_ref[...])).astype(BF16), g2_ref[...])
    r_out[...] = _bdot(mixed(0), wrkv_ref[:, 0:C])
    v_out[...] = _bdot(mixed(3), wrkv_ref[:, 2 * C:])

    z = -(w0_ref[...] + w_lora)
    softplus = jnp.maximum(z, 0.0) + jnp.log1p(jnp.exp(-jnp.abs(z)))
    lw_out[...] = -jnp.exp(-softplus - 0.5)
    a = jax.nn.sigmoid(a0_ref[...] + a_lora)
    kk = k * kk_ref[...]
    k_out[...] = k * (1.0 + (a - 1.0) * ka_ref[...])

    lane = lax.broadcasted_iota(jnp.int32, (tm, LANES), 1)
    first = lane < HEAD
    for s in range(C // LANES):
        sl = slice(s * LANES, (s + 1) * LANES)
        kks = kk[:, sl]
        kkn = kks * lax.rsqrt(jnp.maximum(_half_sums(kks * kks, first), 1e-24))
        an_out[:, sl] = -kkn
        bn_out[:, sl] = kkn * a[:, sl]


def _rwkv_scan_kernel(r_ref, lw_ref, k_ref, v_ref, a_ref, b_ref, g_ref, rk_ref, lg_ref, lb_ref, o_ref,
                      h_ref, qe_s, yl_s, mc_s, gc_s, bonus_s, gate_s, *, n_chunks):
    t = pl.program_id(2)
    last = pl.num_programs(2) - 1
    carried = (qe_s, yl_s, mc_s, gc_s, bonus_s, gate_s)
    inputs = (r_ref, lw_ref, k_ref, v_ref, a_ref, b_ref, g_ref, rk_ref)

    def chain():
        return _rwkv_chain(lg_ref, lb_ref, o_ref, h_ref, *carried, n_chunks=n_chunks)

    @pl.when(t == 0)
    def _():
        h_ref[...] = jnp.zeros_like(h_ref)
        _rwkv_chunks(*inputs, *carried, n_chunks=n_chunks, chain=iter(()))

    @pl.when((t > 0) & (t < last))
    def _():
        _rwkv_chunks(*inputs, *carried, n_chunks=n_chunks, chain=chain())

    @pl.when(t == last)
    def _():
        for _ in chain():
            pass


def _rwkv_chunks(r_ref, lw_ref, k_ref, v_ref, a_ref, b_ref, g_ref, rk_ref, qe_s, yl_s, mc_s, gc_s, bonus_s, gate_s,
                 *, n_chunks, chain):
    L = CHUNK

    row = lax.broadcasted_iota(jnp.int32, (2 * L, LANES), 0)
    col = lax.broadcasted_iota(jnp.int32, (2 * L, LANES), 1)
    block_diag = (row >= L) == (col >= L)
    rt = row & (L - 1)
    ct = col & (L - 1)
    strict = ct < rt
    incl = ct <= rt
    eye = (row == col).astype(F32)
    lane = lax.broadcasted_iota(jnp.int32, (L, LANES), 1)
    first = lane < HEAD
    tr = lax.broadcasted_iota(jnp.int32, (L, L), 0)
    tc = lax.broadcasted_iota(jnp.int32, (L, L), 1)
    tri = (tc <= tr).astype(BF16)

    def stack(z):
        return jnp.concatenate([jnp.where(first, z, 0.0), jnp.where(first, 0.0, z)], axis=0)

    def split3(z):
        hi = z.astype(BF16)
        r1 = z - hi.astype(F32)
        mid = r1.astype(BF16)
        return hi, mid, (r1 - mid.astype(F32)).astype(BF16)

    rk = rk_ref[...]

    chunks = range(n_chunks)
    sls = [pl.ds(c * L, L) for c in chunks]
    R = [r_ref[0, sl, :] for sl in sls]
    LW = [lw_ref[0, sl, :] for sl in sls]
    K = [k_ref[0, sl, :] for sl in sls]
    V = [v_ref[0, sl, :] for sl in sls]
    A = [a_ref[0, sl, :] for sl in sls]
    Bv = [b_ref[0, sl, :] for sl in sls]
    def stage(fn, ticks=1):
        out = [fn(c) for c in chunks]
        for _ in range(ticks):
            next(chain, None)
        return out

    cs = stage(lambda c: _bdot(tri, jnp.concatenate(split3(LW[c]), axis=1)), ticks=2)
    cum = [z[:, :LANES] + z[:, LANES:2 * LANES] + z[:, 2 * LANES:] for z in cs]
    cum_last = [z[L - 1:L, :] for z in cum]
    inv = [jnp.exp(-z) for z in cum]
    to_end = [jnp.exp(cum_last[c] - cum[c]) for c in chunks]
    Rt = [R[c] * jnp.exp(cum[c]) for c in chunks]
    At = [A[c] * jnp.exp(cum[c] - LW[c]) for c in chunks]
    Kt = [K[c] * inv[c] for c in chunks]
    Bt = [Bv[c] * inv[c] for c in chunks]
    AtS = [stack(z) for z in At]
    RtS = [stack(z) for z in Rt]
    BhT = [stack(Bv[c] * to_end[c]).T.astype(BF16) for c in chunks]
    KhT = [stack(K[c] * to_end[c]).T.astype(BF16) for c in chunks]
    kv = stage(lambda c: _bdot(KhT[c], stack(V[c]).astype(BF16)), ticks=2)

    p0 = stage(lambda c: _dot_nt(jnp.concatenate([AtS[c][:L], RtS[c][:L]], axis=0).astype(BF16),
                                 jnp.concatenate([Bt[c], Kt[c]], axis=0).astype(BF16)), ticks=2)
    p1 = stage(lambda c: _dot_nt(jnp.concatenate([AtS[c][L:], RtS[c][L:]], axis=0).astype(BF16),
                                 jnp.concatenate([Kt[c], Bt[c]], axis=0).astype(BF16)), ticks=2)
    top = [jnp.concatenate([p0[c][:L], p1[c][:L]], axis=0) for c in chunks]
    bot = [jnp.concatenate([p0[c][L:], p1[c][L:]], axis=0) for c in chunks]
    n_ab = [jnp.where(block_diag & strict, z, 0.0) for z in top]
    m_ak = [jnp.where(block_diag | ~strict, 0.0, z) for z in top]
    m_rb = [jnp.where(block_diag & incl, z, 0.0) for z in bot]
    m_rk = [jnp.where(block_diag | ~incl, 0.0, z) for z in bot]

    tinv = [eye + z for z in n_ab]
    xb = [z.astype(BF16) for z in n_ab]
    xb = [z.astype(BF16) for z in stage(lambda c: _bdot(xb[c], xb[c]))]
    for _ in range(4):
        xt = stage(lambda c: _bdot(xb[c], jnp.concatenate([xb[c], tinv[c].astype(BF16)], axis=1)))
        xb = [z[:, :LANES].astype(BF16) for z in xt]
        tinv = [tinv[c] + xt[c][:, LANES:] for c in chunks]
    last = stage(lambda c: _bdot(xb[c], tinv[c].astype(BF16)))
    tinv = [tinv[c] + last[c] for c in chunks]

    Vb = [z.astype(BF16) for z in V]
    zy = stage(lambda c: _bdot(jnp.concatenate([m_ak[c], m_rk[c]], axis=0).astype(BF16),
                               jnp.concatenate([Vb[c], Vb[c]], axis=0)))
    z_ak = [jnp.where(block_diag, z[:2 * L], 0.0) for z in zy]
    y_rk = [jnp.where(block_diag, z[2 * L:], 0.0) for z in zy]
    wub = [z.astype(BF16) for z in stage(
        lambda c: _bdot(tinv[c].astype(BF16), jnp.concatenate([AtS[c], z_ak[c]], axis=1).astype(BF16)))]
    qy = stage(lambda c: _bdot(m_rb[c].astype(BF16), wub[c]))
    q_eff = [(RtS[c] + qy[c][:, :LANES]).astype(BF16) for c in chunks]
    y_loc = [y_rk[c] + qy[c][:, LANES:] for c in chunks]
    mg = stage(lambda c: _bdot(BhT[c], wub[c]))
    m_c = [(eye * jnp.exp(cum_last[c]) + mg[c][:, :LANES]).astype(BF16) for c in chunks]
    g_c = [mg[c][:, LANES:] + kv[c] for c in chunks]

    for _ in chain:
        pass
    for c in chunks:
        qe_s[c] = q_eff[c]
        yl_s[c] = y_loc[c]
        mc_s[c] = m_c[c]
        gc_s[c] = g_c[c]
        bonus_s[c] = _half_sums(R[c] * K[c] * rk, first) * V[c]
        gate_s[c] = g_ref[0, sls[c], :]


def _rwkv_chain(lg_ref, lb_ref, o_ref, h_ref, qe_s, yl_s, mc_s, gc_s, bonus_s, gate_s, *, n_chunks):
    L = CHUNK
    lane = lax.broadcasted_iota(jnp.int32, (L, LANES), 1)
    first = lane < HEAD
    lg = lg_ref[...]
    lb = lb_ref[...]
    h = h_ref[...]
    ys = []
    for c in range(n_chunks):
        both = _bdot(jnp.concatenate([qe_s[c], mc_s[c]], axis=0), h.astype(BF16))
        ys.append(both[:2 * L] + yl_s[c])
        h = both[2 * L:] + gc_s[c]
        yield
    h_ref[...] = h

    for c in range(n_chunks):
        y = ys[c][:L] + ys[c][L:]
        mu = _half_sums(y, first) * (1.0 / HEAD)
        d = y - mu
        var = _half_sums(d * d, first) * (1.0 / HEAD)
        yn = d * lax.rsqrt(var + RW_LNX_EPS) * lg + lb
        o_ref[0, pl.ds(c * L, L), :] = ((yn + bonus_s[c]) * gate_s[c]).astype(BF16)


def _rwkv_time_mix(x, B, T, g, mix, w_rkv, w0, w1, w2, a0, a1, a2, g1, g2, k_k, k_a, r_k, lnx_g, lnx_b,
                   *, tm=512, tt=1024):
    M, C = x.shape
    row = pl.BlockSpec((tm, C), lambda i: (i, 0))
    prev = pl.BlockSpec((SUBLANES, C), lambda i: (jnp.maximum(i * (tm // SUBLANES) - 1, 0), 0))
    vec = lambda a: a.reshape(1, C)
    outs = pl.pallas_call(
        functools.partial(_rwkv_prep_kernel, tiles_per_seq=T // tm),
        grid=(M // tm,),
        in_specs=[row, prev, _resident((1, C)), _resident(mix.shape), _resident(w_rkv.shape),
                  _resident(w1.shape), _resident(w2.shape), _resident(a1.shape), _resident(a2.shape),
                  _resident(g1.shape), _resident(g2.shape)] + [_resident((1, C))] * 4,
        out_specs=[row] * 7,
        out_shape=[jax.ShapeDtypeStruct((M, C), F32)] * 7,
        compiler_params=_params("parallel"),
        name="rwkv_prep",
    )(x, x, vec(g), mix, w_rkv, w1, w2, a1, a2, g1, g2, vec(w0), vec(a0), vec(k_k), vec(k_a))
    seq = [z.reshape(B, T, C) for z in outs]
    nt = T // tt
    n_chunks = tt // CHUNK
    slab_in = pl.BlockSpec((1, tt, LANES), lambda b, s, t: (b, jnp.minimum(t, nt - 1), s))
    slab_out = pl.BlockSpec((1, tt, LANES), lambda b, s, t: (b, jnp.maximum(t - 1, 0), s))
    pvec = pl.BlockSpec((1, LANES), lambda b, s, t: (0, s))
    stacked = (n_chunks, 2 * CHUNK, LANES)
    y = pl.pallas_call(
        functools.partial(_rwkv_scan_kernel, n_chunks=n_chunks),
        grid=(B, C // LANES, nt + 1),
        in_specs=[slab_in] * 7 + [pvec] * 3,
        out_specs=slab_out,
        out_shape=jax.ShapeDtypeStruct((B, T, C), BF16),
        scratch_shapes=[pltpu.VMEM((LANES, LANES), F32),
                        pltpu.VMEM(stacked, BF16), pltpu.VMEM(stacked, F32),
                        pltpu.VMEM(stacked, BF16), pltpu.VMEM(stacked, F32),
                        pltpu.VMEM((n_chunks, CHUNK, LANES), F32), pltpu.VMEM((n_chunks, CHUNK, LANES), F32)],
        compiler_params=_params("parallel", "parallel", "arbitrary"),
        name="rwkv_scan",
    )(*seq, vec(r_k), vec(lnx_g), vec(lnx_b))
    return y.reshape(M, C)


def _pool_ffn_kernel(x_ref, xh_ref, gm_ref, pw_ref, sc_ref, g_ref, win_ref, wout_ref, o_ref,
                     *, hidden, tf, tiles_per_seq):
    ti = pl.program_id(0) % tiles_per_seq
    tm, C = x_ref.shape
    group = C // len(POOL_WINDOWS)
    gm = gm_ref[...]
    x = x_ref[...]
    h = _rms(x, gm, NORM_EPS)
    halo = jnp.where(ti == 0, 0.0, _rms(xh_ref[...], gm, NORM_EPS))
    he = jnp.concatenate([halo, h], axis=0)
    t = ti * tm + lax.broadcasted_iota(jnp.int32, (tm, 1), 0)
    ys = []
    for gi, win in enumerate(POOL_WINDOWS):
        sl = slice(gi * group, (gi + 1) * group)
        s = he[:, sl]
        shift = 1
        while shift < win:
            s = s + pltpu.roll(s, shift, 0)
            shift *= 2
        cnt = jnp.minimum(t + 1, win).astype(F32)
        delta = (s[POOL_HALO:] / cnt - h[:, sl]).astype(BF16)
        ys.append(_bdot(delta, pw_ref[gi]))
    x1 = x + jnp.concatenate(ys, axis=1) * sc_ref[...]
    _ffn_residual(x1, g_ref, win_ref, wout_ref, o_ref, hidden, tf)


def _pool_ffn(x, T, g_mix, pool_w, scale, g, w_in, w_out, *, tm=512, tf=256):
    M, C = x.shape
    hidden = w_out.shape[0]
    blocks = tm // POOL_HALO
    row = pl.BlockSpec((tm, C), lambda i: (i, 0))
    return pl.pallas_call(
        functools.partial(_pool_ffn_kernel, hidden=hidden, tf=tf, tiles_per_seq=T // tm),
        grid=(M // tm,),
        in_specs=[row, pl.BlockSpec((POOL_HALO, C), lambda i: (jnp.maximum(i * blocks - 1, 0), 0)),
                  _resident((1, C)), _resident(pool_w.shape), _resident((1, C)), _resident((1, C)),
                  _resident((C, 2 * hidden)), _resident((hidden, C))],
        out_specs=row,
        out_shape=jax.ShapeDtypeStruct((M, C), F32),
        compiler_params=_params("parallel"),
        name="pool_ffn",
    )(x, x, g_mix.reshape(1, C), pool_w, scale.reshape(1, C), g.reshape(1, C), w_in, w_out)


def kernel(x, mix_norm, ffn_norm, ffn_w_in, ffn_w_out, da_wqkv, da_wo, da_q_gain, da_k_gain, da_lambda, da_subln, rw_mix, rw_wrkv, rw_wo, rw_w0, rw_w1, rw_w2, rw_a0, rw_a1, rw_a2, rw_g1, rw_g2, rw_kk, rw_ka, rw_rk, rw_lnx_g, rw_lnx_b, pool_w, pool_scale):
    B, T, C = x.shape
    depth = mix_norm.shape[0]
    bf = lambda w: w.astype(BF16)
    xs = x.reshape(B * T, C)
    ia = ir = ip = 0
    for layer in range(depth):
        kind = layer % N_MIXERS
        ffn = (ffn_norm[layer], bf(ffn_w_in[layer]), bf(ffn_w_out[layer]))
        if kind == 0:
            lambda_init = 0.8 - 0.6 * math.exp(-0.3 * layer)
            y = _diff_attention(xs, B, T, mix_norm[layer], bf(da_wqkv[ia]), da_q_gain[ia], da_k_gain[ia],
                                da_lambda[ia], da_subln[ia], lambda_init)
            xs = _mix_ffn(xs, y, bf(da_wo[ia]), *ffn)
            ia += 1
        elif kind == 1:
            y = _rwkv_time_mix(xs, B, T, mix_norm[layer], rw_mix[ir], bf(rw_wrkv[ir]), rw_w0[ir], bf(rw_w1[ir]),
                               bf(rw_w2[ir]), rw_a0[ir], bf(rw_a1[ir]), bf(rw_a2[ir]), bf(rw_g1[ir]),
                               bf(rw_g2[ir]), rw_kk[ir], rw_ka[ir], rw_rk[ir], rw_lnx_g[ir], rw_lnx_b[ir])
            xs = _mix_ffn(xs, y, bf(rw_wo[ir]), *ffn)
            ir += 1
        else:
            xs = _pool_ffn(xs, T, mix_norm[layer], bf(pool_w[ip]), pool_scale[ip], *ffn)
            ip += 1
    return xs.reshape(B, T, C)
```

```python
import functools
import math

import jax
import jax.numpy as jnp
from jax import lax
from jax.experimental import pallas as pl
from jax.experimental.pallas import tpu as pltpu

F32 = jnp.float32
BF16 = jnp.bfloat16

LANES = 128
SUBLANES = 8
VMEM_LIMIT_BYTES = 48 * 1024 * 1024

CHUNK = 64
HEAD = 64
N_MIXERS = 3
NORM_EPS = 1e-6
QK_EPS = 1e-6
SUBLN_EPS = 1e-5
ROPE_THETA = 10000.0
RW_LNX_EPS = 64e-5
POOL_WINDOWS = (2, 4, 8, 16)
POOL_HALO = 16
NEG = -1e30


def _rms(z, gain, eps):
    return z * lax.rsqrt(jnp.mean(z * z, axis=-1, keepdims=True) + eps) * gain


def _bdot(a, b):
    return jnp.dot(a, b, preferred_element_type=F32)


def _dot_nt(a, b):
    return lax.dot_general(a, b, (((1,), (1,)), ((), ())), preferred_element_type=F32)


def _dot_tn(a, b):
    return lax.dot_general(a, b, (((0,), (0,)), ((), ())), preferred_element_type=F32)


def _half_sums(z, first):
    lo = jnp.sum(jnp.where(first, z, 0.0), axis=-1, keepdims=True)
    hi = jnp.sum(jnp.where(first, 0.0, z), axis=-1, keepdims=True)
    return jnp.where(first, lo, hi)


def _params(*sem):
    return pltpu.CompilerParams(dimension_semantics=sem, vmem_limit_bytes=VMEM_LIMIT_BYTES)


def _resident(shape):
    return pl.BlockSpec(shape, lambda *_: (0,) * len(shape), pipeline_mode=pl.Buffered(1))


def _ffn_residual(x1, g_ref, win_ref, wout_ref, o_ref, hidden, tf):
    hn = _rms(x1, g_ref[...], NORM_EPS).astype(BF16)
    acc = x1
    for f in range(hidden // tf):
        gate = _bdot(hn, win_ref[:, f * tf:(f + 1) * tf])
        up = _bdot(hn, win_ref[:, hidden + f * tf:hidden + (f + 1) * tf])
        act = (gate * jax.nn.sigmoid(gate) * up).astype(BF16)
        acc = acc + _bdot(act, wout_ref[f * tf:(f + 1) * tf, :])
    o_ref[...] = acc


def _mix_ffn_kernel(x_ref, y_ref, wm_ref, g_ref, win_ref, wout_ref, o_ref, *, hidden, tf):
    x1 = x_ref[...] + _bdot(y_ref[...], wm_ref[...])
    _ffn_residual(x1, g_ref, win_ref, wout_ref, o_ref, hidden, tf)


def _mix_ffn(x, y, w_mix, g, w_in, w_out, *, tm=512, tf=256):
    M, C = x.shape
    hidden = w_out.shape[0]
    row = pl.BlockSpec((tm, C), lambda i: (i, 0))
    return pl.pallas_call(
        functools.partial(_mix_ffn_kernel, hidden=hidden, tf=tf),
        grid=(M // tm,),
        in_specs=[row, row, _resident((C, C)), _resident((1, C)), _resident((C, 2 * hidden)),
                  _resident((hidden, C))],
        out_specs=row,
        out_shape=jax.ShapeDtypeStruct((M, C), F32),
        compiler_params=_params("parallel"),
        name="mix_ffn",
    )(x, y, w_mix, g.reshape(1, C), w_in, w_out)


def _qkv_kernel(x_ref, g_ref, wqt_ref, wk_ref, wvt_ref, qgt_ref, kg_ref, cos_ref, sin_ref, cost_ref, sint_ref,
                qt_ref, k_ref, vt_ref):
    tm, C = x_ref.shape
    hn = _rms(x_ref[...], g_ref[...], NORM_EPS).astype(BF16)

    k = _bdot(hn, wk_ref[...])
    vt_ref[0, 0] = _dot_nt(wvt_ref[...], hn).astype(BF16)

    q3 = _dot_nt(wqt_ref[...], hn).reshape(C // HEAD, HEAD, tm)
    ms = jnp.mean(q3 * q3, axis=1, keepdims=True)
    qn = q3 * lax.rsqrt(ms + QK_EPS) * qgt_ref[...][None]
    partner = jnp.concatenate([qn[:, HEAD // 2:], qn[:, :HEAD // 2]], axis=1)
    q3 = qn * cost_ref[...][None] + partner * sint_ref[...][None]
    qt_ref[0, 0] = q3.reshape(C, tm).astype(BF16)

    cos = cos_ref[...]
    sin = sin_ref[...]
    lane = lax.broadcasted_iota(jnp.int32, cos.shape, 1)
    first = lane < HEAD
    low_half = (lane & (HEAD - 1)) < HEAD // 2
    for s in range(C // LANES):
        sl = slice(s * LANES, (s + 1) * LANES)
        z = k[:, sl]
        zn = z * lax.rsqrt(_half_sums(z * z, first) * (1.0 / HEAD) + QK_EPS) * kg_ref[...]
        partner = jnp.where(low_half, pltpu.roll(zn, LANES - HEAD // 2, 1), pltpu.roll(zn, HEAD // 2, 1))
        k_ref[:, sl] = (zn * cos + partner * sin).astype(BF16)


def _attn_kernel(qt_ref, k_ref, vt_ref, lam_ref, sg_ref, o_ref, s_refs, m_refs, l_refs, acc_refs,
                 *, tq, lambda_init):
    i = pl.program_id(2)
    feat = lax.broadcasted_iota(jnp.int32, (LANES, tq), 0)

    def stacked_q(h):
        qt = qt_ref[0, 0, h * LANES:(h + 1) * LANES, :]
        zero = jnp.zeros_like(qt)
        return jnp.concatenate([jnp.where(feat < HEAD, qt, zero), jnp.where(feat < HEAD, zero, qt)], axis=1)

    qst = [stacked_q(0), stacked_q(1)]

    def scores(h, j, diagonal=False):
        start = pl.multiple_of(j * tq, tq)
        s = _bdot(k_ref[0, pl.ds(start, tq), h * LANES:(h + 1) * LANES], qst[h])
        if diagonal:
            key = lax.broadcasted_iota(jnp.int32, s.shape, 0)
            qry = lax.broadcasted_iota(jnp.int32, s.shape, 1)
            qry = jnp.where(qry >= tq, qry - tq, qry)
            s = jnp.where((key // CHUNK) <= (qry // CHUNK), s, NEG)
        s_refs[h] = s

    def absorb(h, j):
        m = m_refs[h]
        m_new = jnp.maximum(m, jnp.max(s_refs[h], axis=0, keepdims=True))
        alpha = jnp.exp2(m - m_new)
        p = jnp.exp2(s_refs[h] - m_new)
        m_refs[h] = m_new
        l_refs[h] = alpha * l_refs[h] + jnp.sum(p, axis=0, keepdims=True)
        vt = vt_ref[0, j, h * LANES:(h + 1) * LANES, :]
        acc_refs[h] = alpha * acc_refs[h] + _bdot(vt, p.astype(BF16))

    def finish(h):
        o = acc_refs[h] / l_refs[h]
        lv = lam_ref[...]
        lam = (jnp.exp(jnp.sum(lv[0:1] * lv[1:2], axis=-1, keepdims=True))
               - jnp.exp(jnp.sum(lv[2:3] * lv[3:4], axis=-1, keepdims=True)) + lambda_init)
        d = (o[:, :tq] - lam * o[:, tq:]).T
        o_ref[0, :, h * LANES:(h + 1) * LANES] = (_rms(d, sg_ref[...], SUBLN_EPS)
                                                  * (1.0 - lambda_init)).astype(BF16)

    m_refs[...] = jnp.full(m_refs.shape, NEG, F32)
    l_refs[...] = jnp.zeros(l_refs.shape, F32)
    acc_refs[...] = jnp.zeros(acc_refs.shape, F32)
    scores(0, i, diagonal=True)
    scores(1, i, diagonal=True)
    absorb(0, i)

    def step(n, _):
        scores(0, n)
        absorb(1, jnp.where(n == 0, i, n - 1))
        scores(1, n)
        absorb(0, n)
        return 0

    lax.fori_loop(0, i, step, 0)
    absorb(1, jnp.where(i == 0, i, i - 1))
    finish(0)
    finish(1)


def _rope_tables(T):
    half = HEAD // 2
    inv = 1.0 / (ROPE_THETA ** (jnp.arange(0, HEAD, 2, dtype=F32) / HEAD))
    ang = jnp.arange(T, dtype=F32)[:, None] * inv[None, :]
    cos, sin = jnp.cos(ang), jnp.sin(ang)
    cos_h = jnp.concatenate([cos, cos], axis=1)
    sin_h = jnp.concatenate([-sin, sin], axis=1)
    reps = LANES // HEAD
    return jnp.tile(cos_h, (1, reps)), jnp.tile(sin_h, (1, reps)), cos_h.T, sin_h.T


def _diff_attention(x, B, T, g, w_qkv, q_gain, k_gain, lam_vec, subln_g, lambda_init, *, tq=512):
    M, C = x.shape
    H = C // LANES
    nt = T // tq
    cos_t, sin_t, cos_tt, sin_tt = _rope_tables(T)
    qgt = jnp.broadcast_to((q_gain * (HEAD ** -0.5 * math.log2(math.e)))[:, None], (HEAD, tq))
    kg = jnp.tile(k_gain, LANES // HEAD).reshape(1, LANES)
    wqt, wk, wvt = w_qkv[:, :C].T, w_qkv[:, C:2 * C], w_qkv[:, 2 * C:].T
    row = pl.BlockSpec((tq, C), lambda i: (i, 0))
    table = pl.BlockSpec((tq, LANES), lambda i: (i % nt, 0))
    table_t = pl.BlockSpec((HEAD, tq), lambda i: (0, i % nt))
    transposed = pl.BlockSpec((1, 1, C, tq), lambda i: (i // nt, i % nt, 0, 0))
    qt, k, vt = pl.pallas_call(
        _qkv_kernel,
        grid=(M // tq,),
        in_specs=[row, _resident((1, C)), _resident((C, C)), _resident((C, C)), _resident((C, C)),
                  _resident((HEAD, tq)), _resident((1, LANES)), table, table, table_t, table_t],
        out_specs=[transposed, row, transposed],
        out_shape=[jax.ShapeDtypeStruct((B, nt, C, tq), BF16), jax.ShapeDtypeStruct((M, C), BF16),
                   jax.ShapeDtypeStruct((B, nt, C, tq), BF16)],
        compiler_params=_params("parallel"),
        name="attn_qkv",
    )(x, g.reshape(1, C), wqt, wk, wvt, qgt, kg, cos_t, sin_t, cos_tt, sin_tt)
    o = pl.pallas_call(
        functools.partial(_attn_kernel, tq=tq, lambda_init=lambda_init),
        grid=(B, H // 2, nt),
        in_specs=[pl.BlockSpec((1, 1, 2 * LANES, tq), lambda b, h, i: (b, i, h, 0)),
                  pl.BlockSpec((1, T, 2 * LANES), lambda b, h, i: (b, 0, h)),
                  pl.BlockSpec((1, nt, 2 * LANES, tq), lambda b, h, i: (b, 0, h, 0)),
                  _resident((4, HEAD)), _resident((1, LANES))],
        out_specs=pl.BlockSpec((1, tq, 2 * LANES), lambda b, h, i: (b, i, h)),
        out_shape=jax.ShapeDtypeStruct((B, T, C), BF16),
        scratch_shapes=[pltpu.VMEM((2, tq, 2 * tq), F32), pltpu.VMEM((2, 1, 2 * tq), F32),
                        pltpu.VMEM((2, 1, 2 * tq), F32), pltpu.VMEM((2, LANES, 2 * tq), F32)],
        compiler_params=_params("parallel", "parallel", "arbitrary"),
        name="attn_core",
    )(qt, k.reshape(B, T, C), vt, lam_vec, subln_g.reshape(1, LANES))
    return o.reshape(M, C)


def _rwkv_prep_kernel(x_ref, xp_ref, g_ref, mix_ref, wrkv_ref, w1_ref, w2_ref, a1_ref, a2_ref, g1_ref, g2_ref,
                      w0_ref, a0_ref, kk_ref, ka_ref,
                      r_out, lw_out, k_out, v_out, an_out, bn_out, g_out, *, tiles_per_seq):
    i = pl.program_id(0)
    tm, C = x_ref.shape
    g = g_ref[...]
    h = _rms(x_ref[...], g, NORM_EPS)
    h_last = _rms(xp_ref[SUBLANES - 1:SUBLANES, :], g, NORM_EPS)
    h_last = jnp.where(i % tiles_per_seq == 0, 0.0, h_last)
    row = lax.broadcasted_iota(jnp.int32, h.shape, 0)
    dx = jnp.where(row == 0, h_last, pltpu.roll(h, 1, 0)) - h
    mix = mix_ref[...]

    def mixed(n):
        return (h + dx * mix[n:n + 1]).astype(BF16)

    a_lora = _bdot(_bdot(mixed(4), a1_ref[...]).astype(BF16), a2_ref[...])
    k = _bdot(mixed(2), wrkv_ref[:, C:2 * C])
    w_lora = _bdot(jnp.tanh(_bdot(mixed(1), w1_ref[...])).astype(BF16), w2_ref[...])
    g_out[...] = _bdot(jax.nn.sigmoid(_bdot(mixed(5), g1_ref[...])).astype(BF16), g2_ref[...])
    r_out[...] = _bdot(mixed(0), wrkv_ref[:, 0:C])
    v_out[...] = _bdot(mixed(3), wrkv_ref[:, 2 * C:])

    lw_out[...] = -math.exp(-0.5) * jax.nn.sigmoid(w0_ref[...] + w_lora)
    a = jax.nn.sigmoid(a0_ref[...] + a_lora)
    kk = k * kk_ref[...]
    k_out[...] = k * (1.0 + (a - 1.0) * ka_ref[...])

    lane = lax.broadcasted_iota(jnp.int32, (tm, LANES), 1)
    first = lane < HEAD
    for s in range(C // LANES):
        sl = slice(s * LANES, (s + 1) * LANES)
        kks = kk[:, sl]
        kkn = kks * lax.rsqrt(jnp.maximum(_half_sums(kks * kks, first), 1e-24))
        an_out[:, sl] = -kkn
        bn_out[:, sl] = kkn * a[:, sl]


def _rwkv_scan_kernel(r_ref, lw_ref, k_ref, v_ref, a_ref, b_ref, g_ref, rk_ref, lg_ref, lb_ref, o_ref,
                      h_ref, qe_s, yl_s, mc_s, gc_s, bonus_s, gate_s, *, n_chunks):
    t = pl.program_id(2)
    last = pl.num_programs(2) - 1
    carried = (qe_s, yl_s, mc_s, gc_s, bonus_s, gate_s)
    inputs = (r_ref, lw_ref, k_ref, v_ref, a_ref, b_ref, g_ref, rk_ref)

    def chain():
        return _rwkv_chain(lg_ref, lb_ref, o_ref, h_ref, *carried, n_chunks=n_chunks)

    @pl.when(t == 0)
    def _():
        h_ref[...] = jnp.zeros_like(h_ref)
        _rwkv_chunks(*inputs, *carried, n_chunks=n_chunks, chain=iter(()))

    @pl.when((t > 0) & (t < last))
    def _():
        _rwkv_chunks(*inputs, *carried, n_chunks=n_chunks, chain=chain())

    @pl.when(t == last)
    def _():
        for _ in chain():
            pass


def _rwkv_chunks(r_ref, lw_ref, k_ref, v_ref, a_ref, b_ref, g_ref, rk_ref, qe_s, yl_s, mc_s, gc_s, bonus_s, gate_s,
                 *, n_chunks, chain):
    L = CHUNK

    row = lax.broadcasted_iota(jnp.int32, (2 * L, LANES), 0)
    col = lax.broadcasted_iota(jnp.int32, (2 * L, LANES), 1)
    block_diag = (row >= L) == (col >= L)
    rt = row & (L - 1)
    ct = col & (L - 1)
    strict = ct < rt
    incl = ct <= rt
    eye = (row == col).astype(F32)
    lane = lax.broadcasted_iota(jnp.int32, (L, LANES), 1)
    first = lane < HEAD
    tr = lax.broadcasted_iota(jnp.int32, (L, L), 0)
    tc = lax.broadcasted_iota(jnp.int32, (L, L), 1)
    tri = (tc <= tr).astype(BF16)

    def stack(z):
        return jnp.concatenate([jnp.where(first, z, 0.0), jnp.where(first, 0.0, z)], axis=0)

    def split3(z):
        hi = z.astype(BF16)
        r1 = z - hi.astype(F32)
        mid = r1.astype(BF16)
        return hi, mid, (r1 - mid.astype(F32)).astype(BF16)

    rk = rk_ref[...]

    chunks = range(n_chunks)
    sls = [pl.ds(c * L, L) for c in chunks]
    R = [r_ref[0, sl, :] for sl in sls]
    LW = [lw_ref[0, sl, :] for sl in sls]
    K = [k_ref[0, sl, :] for sl in sls]
    V = [v_ref[0, sl, :] for sl in sls]
    A = [a_ref[0, sl, :] for sl in sls]
    Bv = [b_ref[0, sl, :] for sl in sls]
    def stage(fn, ticks=1):
        out = [fn(c) for c in chunks]
        for _ in range(ticks):
            next(chain, None)
        return out

    cs = stage(lambda c: _bdot(tri, jnp.concatenate(split3(LW[c]), axis=1)), ticks=2)
    cum = [z[:, :LANES] + z[:, LANES:2 * LANES] + z[:, 2 * LANES:] for z in cs]
    cum_last = [z[L - 1:L, :] for z in cum]
    inv = [jnp.exp(-z) for z in cum]
    to_end = [jnp.exp(cum_last[c] - cum[c]) for c in chunks]
    Rt = [R[c] * jnp.exp(cum[c]) for c in chunks]
    At = [A[c] * jnp.exp(cum[c] - LW[c]) for c in chunks]
    Kt = [K[c] * inv[c] for c in chunks]
    Bt = [Bv[c] * inv[c] for c in chunks]
    AtS = [stack(z) for z in At]
    RtS = [stack(z) for z in Rt]
    BhT = [stack(Bv[c] * to_end[c]).T.astype(BF16) for c in chunks]
    KhT = [stack(K[c] * to_end[c]).T.astype(BF16) for c in chunks]
    kv = stage(lambda c: _bdot(KhT[c], stack(V[c]).astype(BF16)), ticks=2)

    p0 = stage(lambda c: _dot_nt(jnp.concatenate([AtS[c][:L], RtS[c][:L]], axis=0).astype(BF16),
                                 jnp.concatenate([Bt[c], Kt[c]], axis=0).astype(BF16)), ticks=2)
    p1 = stage(lambda c: _dot_nt(jnp.concatenate([AtS[c][L:], RtS[c][L:]], axis=0).astype(BF16),
                                 jnp.concatenate([Kt[c], Bt[c]], axis=0).astype(BF16)), ticks=2)
    top = [jnp.concatenate([p0[c][:L], p1[c][:L]], axis=0) for c in chunks]
    bot = [jnp.concatenate([p0[c][L:], p1[c][L:]], axis=0) for c in chunks]
    n_ab = [jnp.where(block_diag & strict, z, 0.0) for z in top]
    m_ak = [jnp.where(block_diag | ~strict, 0.0, z) for z in top]
    m_rb = [jnp.where(block_diag & incl, z, 0.0) for z in bot]
    m_rk = [jnp.where(block_diag | ~incl, 0.0, z) for z in bot]

    tinv = [eye + z for z in n_ab]
    xb = [z.astype(BF16) for z in n_ab]
    xb = [z.astype(BF16) for z in stage(lambda c: _bdot(xb[c], xb[c]))]
    for _ in range(4):
        xt = stage(lambda c: _bdot(xb[c], jnp.concatenate([xb[c], tinv[c].astype(BF16)], axis=1)))
        xb = [z[:, :LANES].astype(BF16) for z in xt]
        tinv = [tinv[c] + xt[c][:, LANES:] for c in chunks]
    last = stage(lambda c: _bdot(xb[c], tinv[c].astype(BF16)))
    tinv = [tinv[c] + last[c] for c in chunks]

    Vb = [z.astype(BF16) for z in V]
    zy = stage(lambda c: _bdot(jnp.concatenate([m_ak[c], m_rk[c]], axis=0).astype(BF16),
                               jnp.concatenate([Vb[c], Vb[c]], axis=0)))
    z_ak = [jnp.where(block_diag, z[:2 * L], 0.0) for z in zy]
    y_rk = [jnp.where(block_diag, z[2 * L:], 0.0) for z in zy]
    wub = [z.astype(BF16) for z in stage(
        lambda c: _bdot(tinv[c].astype(BF16), jnp.concatenate([AtS[c], z_ak[c]], axis=1).astype(BF16)))]
    qy = stage(lambda c: _bdot(m_rb[c].astype(BF16), wub[c]))
    q_eff = [(RtS[c] + qy[c][:, :LANES]).astype(BF16) for c in chunks]
    y_loc = [y_rk[c] + qy[c][:, LANES:] for c in chunks]
    mg = stage(lambda c: _bdot(BhT[c], wub[c]))
    m_c = [(eye * jnp.exp(cum_last[c]) + mg[c][:, :LANES]).astype(BF16) for c in chunks]
    g_c = [mg[c][:, LANES:] + kv[c] for c in chunks]

    for _ in chain:
        pass
    for c in chunks:
        qe_s[c] = q_eff[c]
        yl_s[c] = y_loc[c]
        mc_s[c] = m_c[c]
        gc_s[c] = g_c[c]
        bonus_s[c] = _half_sums(R[c] * K[c] * rk, first) * V[c]
        gate_s[c] = g_ref[0, sls[c], :]


def _rwkv_chain(lg_ref, lb_ref, o_ref, h_ref, qe_s, yl_s, mc_s, gc_s, bonus_s, gate_s, *, n_chunks):
    L = CHUNK
    lane = lax.broadcasted_iota(jnp.int32, (L, LANES), 1)
    first = lane < HEAD
    lg = lg_ref[...]
    lb = lb_ref[...]
    h = h_ref[...]
    ys = []
    for c in range(n_chunks):
        both = _bdot(jnp.concatenate([qe_s[c], mc_s[c]], axis=0), h.astype(BF16))
        ys.append(both[:2 * L] + yl_s[c])
        h = both[2 * L:] + gc_s[c]
        yield
    h_ref[...] = h

    for c in range(n_chunks):
        y = ys[c][:L] + ys[c][L:]
        mu = _half_sums(y, first) * (1.0 / HEAD)
        d = y - mu
        var = _half_sums(d * d, first) * (1.0 / HEAD)
        yn = d * lax.rsqrt(var + RW_LNX_EPS) * lg + lb
        o_ref[0, pl.ds(c * L, L), :] = ((yn + bonus_s[c]) * gate_s[c]).astype(BF16)


def _rwkv_time_mix(x, B, T, g, mix, w_rkv, w0, w1, w2, a0, a1, a2, g1, g2, k_k, k_a, r_k, lnx_g, lnx_b,
                   *, tm=512, tt=1024):
    M, C = x.shape
    row = pl.BlockSpec((tm, C), lambda i: (i, 0))
    prev = pl.BlockSpec((SUBLANES, C), lambda i: (jnp.maximum(i * (tm // SUBLANES) - 1, 0), 0))
    vec = lambda a: a.reshape(1, C)
    outs = pl.pallas_call(
        functools.partial(_rwkv_prep_kernel, tiles_per_seq=T // tm),
        grid=(M // tm,),
        in_specs=[row, prev, _resident((1, C)), _resident(mix.shape), _resident(w_rkv.shape),
                  _resident(w1.shape), _resident(w2.shape), _resident(a1.shape), _resident(a2.shape),
                  _resident(g1.shape), _resident(g2.shape)] + [_resident((1, C))] * 4,
        out_specs=[row] * 7,
        out_shape=[jax.ShapeDtypeStruct((M, C), F32)] * 7,
        compiler_params=_params("parallel"),
        name="rwkv_prep",
    )(x, x, vec(g), mix, w_rkv, w1, w2, a1, a2, g1, g2, vec(w0), vec(a0), vec(k_k), vec(k_a))
    seq = [z.reshape(B, T, C) for z in outs]
    nt = T // tt
    n_chunks = tt // CHUNK
    slab_in = pl.BlockSpec((1, tt, LANES), lambda b, s, t: (b, jnp.minimum(t, nt - 1), s))
    slab_out = pl.BlockSpec((1, tt, LANES), lambda b, s, t: (b, jnp.maximum(t - 1, 0), s))
    pvec = pl.BlockSpec((1, LANES), lambda b, s, t: (0, s))
    stacked = (n_chunks, 2 * CHUNK, LANES)
    y = pl.pallas_call(
        functools.partial(_rwkv_scan_kernel, n_chunks=n_chunks),
        grid=(B, C // LANES, nt + 1),
        in_specs=[slab_in] * 7 + [pvec] * 3,
        out_specs=slab_out,
        out_shape=jax.ShapeDtypeStruct((B, T, C), BF16),
        scratch_shapes=[pltpu.VMEM((LANES, LANES), F32),
                        pltpu.VMEM(stacked, BF16), pltpu.VMEM(stacked, F32),
                        pltpu.VMEM(stacked, BF16), pltpu.VMEM(stacked, F32),
                        pltpu.VMEM((n_chunks, CHUNK, LANES), F32), pltpu.VMEM((n_chunks, CHUNK, LANES), F32)],
        compiler_params=_params("parallel", "parallel", "arbitrary"),
        name="rwkv_scan",
    )(*seq, vec(r_k), vec(lnx_g), vec(lnx_b))
    return y.reshape(M, C)


def _pool_ffn_kernel(x_ref, xh_ref, gm_ref, pw_ref, sc_ref, g_ref, win_ref, wout_ref, o_ref,
                     *, hidden, tf, tiles_per_seq):
    ti = pl.program_id(0) % tiles_per_seq
    tm, C = x_ref.shape
    group = C // len(POOL_WINDOWS)
    gm = gm_ref[...]
    x = x_ref[...]
    h = _rms(x, gm, NORM_EPS)
    halo = jnp.where(ti == 0, 0.0, _rms(xh_ref[...], gm, NORM_EPS))
    he = jnp.concatenate([halo, h], axis=0)
    t = ti * tm + lax.broadcasted_iota(jnp.int32, (tm, 1), 0)
    ys = []
    for gi, win in enumerate(POOL_WINDOWS):
        sl = slice(gi * group, (gi + 1) * group)
        s = he[:, sl]
        shift = 1
        while shift < win:
            s = s + pltpu.roll(s, shift, 0)
            shift *= 2
        cnt = jnp.minimum(t + 1, win).astype(F32)
        delta = (s[POOL_HALO:] / cnt - h[:, sl]).astype(BF16)
        ys.append(_bdot(delta, pw_ref[gi]))
    x1 = x + jnp.concatenate(ys, axis=1) * sc_ref[...]
    _ffn_residual(x1, g_ref, win_ref, wout_ref, o_ref, hidden, tf)


def _pool_ffn(x, T, g_mix, pool_w, scale, g, w_in, w_out, *, tm=512, tf=256):
    M, C = x.shape
    hidden = w_out.shape[0]
    blocks = tm // POOL_HALO
    row = pl.BlockSpec((tm, C), lambda i: (i, 0))
    return pl.pallas_call(
        functools.partial(_pool_ffn_kernel, hidden=hidden, tf=tf, tiles_per_seq=T // tm),
        grid=(M // tm,),
        in_specs=[row, pl.BlockSpec((POOL_HALO, C), lambda i: (jnp.maximum(i * blocks - 1, 0), 0)),
                  _resident((1, C)), _resident(pool_w.shape), _resident((1, C)), _resident((1, C)),
                  _resident((C, 2 * hidden)), _resident((hidden, C))],
        out_specs=row,
        out_shape=jax.ShapeDtypeStruct((M, C), F32),
        compiler_params=_params("parallel"),
        name="pool_ffn",
    )(x, x, g_mix.reshape(1, C), pool_w, scale.reshape(1, C), g.reshape(1, C), w_in, w_out)


def kernel(x, mix_norm, ffn_norm, ffn_w_in, ffn_w_out, da_wqkv, da_wo, da_q_gain, da_k_gain, da_lambda, da_subln, rw_mix, rw_wrkv, rw_wo, rw_w0, rw_w1, rw_w2, rw_a0, rw_a1, rw_a2, rw_g1, rw_g2, rw_kk, rw_ka, rw_rk, rw_lnx_g, rw_lnx_b, pool_w, pool_scale):
    B, T, C = x.shape
    depth = mix_norm.shape[0]
    bf = lambda w: w.astype(BF16)
    xs = x.reshape(B * T, C)
    ia = ir = ip = 0
    for layer in range(depth):
        kind = layer % N_MIXERS
        ffn = (ffn_norm[layer], bf(ffn_w_in[layer]), bf(ffn_w_out[layer]))
        if kind == 0:
            lambda_init = 0.8 - 0.6 * math.exp(-0.3 * layer)
            y = _diff_attention(xs, B, T, mix_norm[layer], bf(da_wqkv[ia]), da_q_gain[ia], da_k_gain[ia],
                                da_lambda[ia], da_subln[ia], lambda_init)
            xs = _mix_ffn(xs, y, bf(da_wo[ia]), *ffn)
            ia += 1
        elif kind == 1:
            y = _rwkv_time_mix(xs, B, T, mix_norm[layer], rw_mix[ir], bf(rw_wrkv[ir]), rw_w0[ir], bf(rw_w1[ir]),
                               bf(rw_w2[ir]), rw_a0[ir], bf(rw_a1[ir]), bf(rw_a2[ir]), bf(rw_g1[ir]),
                               bf(rw_g2[ir]), rw_kk[ir], rw_ka[ir], rw_rk[ir], rw_lnx_g[ir], rw_lnx_b[ir])
            xs = _mix_ffn(xs, y, bf(rw_wo[ir]), *ffn)
            ir += 1
        else:
            xs = _pool_ffn(xs, T, mix_norm[layer], bf(pool_w[ip]), pool_scale[ip], *ffn)
            ip += 1
    return xs.reshape(B, T, C)
```

```python
import functools
import math

import jax
import jax.numpy as jnp
from jax import lax
from jax.experimental import pallas as pl
from jax.experimental.pallas import tpu as pltpu

F32 = jnp.float32
BF16 = jnp.bfloat16

LANES = 128
SUBLANES = 8
VMEM_LIMIT_BYTES = 48 * 1024 * 1024

CHUNK = 64
HEAD = 64
N_MIXERS = 3
NORM_EPS = 1e-6
QK_EPS = 1e-6
SUBLN_EPS = 1e-5
ROPE_THETA = 10000.0
RW_LNX_EPS = 64e-5
POOL_WINDOWS = (2, 4, 8, 16)
POOL_HALO = 16
NEG = -1e30


def _rms(z, gain, eps):
    return z * lax.rsqrt(jnp.mean(z * z, axis=-1, keepdims=True) + eps) * gain


def _bdot(a, b):
    return jnp.dot(a, b, preferred_element_type=F32)


def _dot_nt(a, b):
    return lax.dot_general(a, b, (((1,), (1,)), ((), ())), preferred_element_type=F32)


def _dot_tn(a, b):
    return lax.dot_general(a, b, (((0,), (0,)), ((), ())), preferred_element_type=F32)


def _half_sums(z, first):
    lo = jnp.sum(jnp.where(first, z, 0.0), axis=-1, keepdims=True)
    hi = jnp.sum(jnp.where(first, 0.0, z), axis=-1, keepdims=True)
    return jnp.where(first, lo, hi)


def _params(*sem):
    return pltpu.CompilerParams(dimension_semantics=sem, vmem_limit_bytes=VMEM_LIMIT_BYTES)


def _resident(shape):
    return pl.BlockSpec(shape, lambda *_: (0,) * len(shape), pipeline_mode=pl.Buffered(1))


def _ffn_residual(x1, g_ref, win_ref, wout_ref, o_ref, hidden, tf):
    hn = _rms(x1, g_ref[...], NORM_EPS).astype(BF16)
    acc = x1
    for f in range(hidden // tf):
        gate = _bdot(hn, win_ref[:, f * tf:(f + 1) * tf])
        up = _bdot(hn, win_ref[:, hidden + f * tf:hidden + (f + 1) * tf])
        act = (gate * jax.nn.sigmoid(gate) * up).astype(BF16)
        acc = acc + _bdot(act, wout_ref[f * tf:(f + 1) * tf, :])
    o_ref[...] = acc


def _mix_ffn_kernel(x_ref, y_ref, wm_ref, g_ref, win_ref, wout_ref, o_ref, *, hidden, tf):
    x1 = x_ref[...] + _bdot(y_ref[...], wm_ref[...])
    _ffn_residual(x1, g_ref, win_ref, wout_ref, o_ref, hidden, tf)


def _mix_ffn(x, y, w_mix, g, w_in, w_out, *, tm=512, tf=256):
    M, C = x.shape
    hidden = w_out.shape[0]
    row = pl.BlockSpec((tm, C), lambda i: (i, 0))
    return pl.pallas_call(
        functools.partial(_mix_ffn_kernel, hidden=hidden, tf=tf),
        grid=(M // tm,),
        in_specs=[row, row, _resident((C, C)), _resident((1, C)), _resident((C, 2 * hidden)),
                  _resident((hidden, C))],
        out_specs=row,
        out_shape=jax.ShapeDtypeStruct((M, C), F32),
        compiler_params=_params("parallel"),
        name="mix_ffn",
    )(x, y, w_mix, g.reshape(1, C), w_in, w_out)


def _qkv_kernel(x_ref, g_ref, wqt_ref, wk_ref, wvt_ref, qgt_ref, kg_ref, cos_ref, sin_ref, cost_ref, sint_ref,
                qt_ref, k_ref, vt_ref):
    tm, C = x_ref.shape
    hn = _rms(x_ref[...], g_ref[...], NORM_EPS).astype(BF16)

    k = _bdot(hn, wk_ref[...])
    vt_ref[0, 0] = _dot_nt(wvt_ref[...], hn).astype(BF16)

    q3 = _dot_nt(wqt_ref[...], hn).reshape(C // HEAD, HEAD, tm)
    ms = jnp.mean(q3 * q3, axis=1, keepdims=True)
    qn = q3 * lax.rsqrt(ms + QK_EPS) * qgt_ref[...][None]
    partner = jnp.concatenate([qn[:, HEAD // 2:], qn[:, :HEAD // 2]], axis=1)
    q3 = qn * cost_ref[...][None] + partner * sint_ref[...][None]
    qt_ref[0, 0] = q3.reshape(C, tm).astype(BF16)

    cos = cos_ref[...]
    sin = sin_ref[...]
    lane = lax.broadcasted_iota(jnp.int32, cos.shape, 1)
    first = lane < HEAD
    low_half = (lane & (HEAD - 1)) < HEAD // 2
    for s in range(C // LANES):
        sl = slice(s * LANES, (s + 1) * LANES)
        z = k[:, sl]
        zn = z * lax.rsqrt(_half_sums(z * z, first) * (1.0 / HEAD) + QK_EPS) * kg_ref[...]
        partner = jnp.where(low_half, pltpu.roll(zn, LANES - HEAD // 2, 1), pltpu.roll(zn, HEAD // 2, 1))
        k_ref[:, sl] = (zn * cos + partner * sin).astype(BF16)


def _attn_kernel(qt_ref, k_ref, vt_ref, lam_ref, sg_ref, o_ref, s_refs, m_refs, l_refs, acc_refs,
                 *, tq, lambda_init):
    i = pl.program_id(2)
    feat = lax.broadcasted_iota(jnp.int32, (LANES, tq), 0)

    def stacked_q(h):
        qt = qt_ref[0, 0, h * LANES:(h + 1) * LANES, :]
        zero = jnp.zeros_like(qt)
        return jnp.concatenate([jnp.where(feat < HEAD, qt, zero), jnp.where(feat < HEAD, zero, qt)], axis=1)

    qst = [stacked_q(0), stacked_q(1)]

    def scores(h, j, diagonal=False):
        start = pl.multiple_of(j * tq, tq)
        s = _bdot(k_ref[0, pl.ds(start, tq), h * LANES:(h + 1) * LANES], qst[h])
        if diagonal:
            key = lax.broadcasted_iota(jnp.int32, s.shape, 0)
            qry = lax.broadcasted_iota(jnp.int32, s.shape, 1)
            qry = jnp.where(qry >= tq, qry - tq, qry)
            s = jnp.where((key // CHUNK) <= (qry // CHUNK), s, NEG)
        s_refs[h] = s

    def absorb(h, j):
        m = m_refs[h]
        m_new = jnp.maximum(m, jnp.max(s_refs[h], axis=0, keepdims=True))
        alpha = jnp.exp2(m - m_new)
        p = jnp.exp2(s_refs[h] - m_new)
        m_refs[h] = m_new
        l_refs[h] = alpha * l_refs[h] + jnp.sum(p, axis=0, keepdims=True)
        vt = vt_ref[0, j, h * LANES:(h + 1) * LANES, :]
        acc_refs[h] = alpha * acc_refs[h] + _bdot(vt, p.astype(BF16))

    def finish(h):
        o = acc_refs[h] / l_refs[h]
        lv = lam_ref[...]
        lam = (jnp.exp(jnp.sum(lv[0:1] * lv[1:2], axis=-1, keepdims=True))
               - jnp.exp(jnp.sum(lv[2:3] * lv[3:4], axis=-1, keepdims=True)) + lambda_init)
        d = (o[:, :tq] - lam * o[:, tq:]).T
        o_ref[0, :, h * LANES:(h + 1) * LANES] = (_rms(d, sg_ref[...], SUBLN_EPS)
                                                  * (1.0 - lambda_init)).astype(BF16)

    m_refs[...] = jnp.full(m_refs.shape, NEG, F32)
    l_refs[...] = jnp.zeros(l_refs.shape, F32)
    acc_refs[...] = jnp.zeros(acc_refs.shape, F32)
    scores(0, i, diagonal=True)
    scores(1, i, diagonal=True)
    absorb(0, i)

    def step(n, _):
        scores(0, n)
        absorb(1, jnp.where(n == 0, i, n - 1))
        scores(1, n)
        absorb(0, n)
        return 0

    lax.fori_loop(0, i, step, 0)
    absorb(1, jnp.where(i == 0, i, i - 1))
    finish(0)
    finish(1)


def _rope_tables(T):
    half = HEAD // 2
    inv = 1.0 / (ROPE_THETA ** (jnp.arange(0, HEAD, 2, dtype=F32) / HEAD))
    ang = jnp.arange(T, dtype=F32)[:, None] * inv[None, :]
    cos, sin = jnp.cos(ang), jnp.sin(ang)
    cos_h = jnp.concatenate([cos, cos], axis=1)
    sin_h = jnp.concatenate([-sin, sin], axis=1)
    reps = LANES // HEAD
    return jnp.tile(cos_h, (1, reps)), jnp.tile(sin_h, (1, reps)), cos_h.T, sin_h.T


def _diff_attention(x, B, T, g, w_qkv, q_gain, k_gain, lam_vec, subln_g, lambda_init, *, tq=512):
    M, C = x.shape
    H = C // LANES
    nt = T // tq
    cos_t, sin_t, cos_tt, sin_tt = _rope_tables(T)
    qgt = jnp.broadcast_to((q_gain * (HEAD ** -0.5 * math.log2(math.e)))[:, None], (HEAD, tq))
    kg = jnp.tile(k_gain, LANES // HEAD).reshape(1, LANES)
    wqt, wk, wvt = w_qkv[:, :C].T, w_qkv[:, C:2 * C], w_qkv[:, 2 * C:].T
    row = pl.BlockSpec((tq, C), lambda i: (i, 0))
    table = pl.BlockSpec((tq, LANES), lambda i: (i % nt, 0))
    table_t = pl.BlockSpec((HEAD, tq), lambda i: (0, i % nt))
    transposed = pl.BlockSpec((1, 1, C, tq), lambda i: (i // nt, i % nt, 0, 0))
    qt, k, vt = pl.pallas_call(
        _qkv_kernel,
        grid=(M // tq,),
        in_specs=[row, _resident((1, C)), _resident((C, C)), _resident((C, C)), _resident((C, C)),
                  _resident((HEAD, tq)), _resident((1, LANES)), table, table, table_t, table_t],
        out_specs=[transposed, row, transposed],
        out_shape=[jax.ShapeDtypeStruct((B, nt, C, tq), BF16), jax.ShapeDtypeStruct((M, C), BF16),
                   jax.ShapeDtypeStruct((B, nt, C, tq), BF16)],
        compiler_params=_params("parallel"),
        name="attn_qkv",
    )(x, g.reshape(1, C), wqt, wk, wvt, qgt, kg, cos_t, sin_t, cos_tt, sin_tt)
    o = pl.pallas_call(
        functools.partial(_attn_kernel, tq=tq, lambda_init=lambda_init),
        grid=(B, H // 2, nt),
        in_specs=[pl.BlockSpec((1, 1, 2 * LANES, tq), lambda b, h, i: (b, i, h, 0)),
                  pl.BlockSpec((1, T, 2 * LANES), lambda b, h, i: (b, 0, h)),
                  pl.BlockSpec((1, nt, 2 * LANES, tq), lambda b, h, i: (b, 0, h, 0)),
                  _resident((4, HEAD)), _resident((1, LANES))],
        out_specs=pl.BlockSpec((1, tq, 2 * LANES), lambda b, h, i: (b, i, h)),
        out_shape=jax.ShapeDtypeStruct((B, T, C), BF16),
        scratch_shapes=[pltpu.VMEM((2, tq, 2 * tq), F32), pltpu.VMEM((2, 1, 2 * tq), F32),
                        pltpu.VMEM((2, 1, 2 * tq), F32), pltpu.VMEM((2, LANES, 2 * tq), F32)],
        compiler_params=_params("parallel", "parallel", "arbitrary"),
        name="attn_core",
    )(qt, k.reshape(B, T, C), vt, lam_vec, subln_g.reshape(1, LANES))
    return o.reshape(M, C)


def _rwkv_prep_kernel(x_ref, xp_ref, g_ref, mix_ref, wrkv_ref, w1_ref, w2_ref, a1_ref, a2_ref, g1_ref, g2_ref,
                      w0_ref, a0_ref, kk_ref, ka_ref,
                      r_out, lw_out, k_out, v_out, an_out, bn_out, g_out, *, tiles_per_seq):
    i = pl.program_id(0)
    tm, C = x_ref.shape
    g = g_ref[...]
    h = _rms(x_ref[...], g, NORM_EPS)
    h_last = _rms(xp_ref[SUBLANES - 1:SUBLANES, :], g, NORM_EPS)
    h_last = jnp.where(i % tiles_per_seq == 0, 0.0, h_last)
    row = lax.broadcasted_iota(jnp.int32, h.shape, 0)
    dx = jnp.where(row == 0, h_last, pltpu.roll(h, 1, 0)) - h
    mix = mix_ref[...]

    def mixed(n):
        return (h + dx * mix[n:n + 1]).astype(BF16)

    a_lora = _bdot(_bdot(mixed(4), a1_ref[...]).astype(BF16), a2_ref[...])
    k = _bdot(mixed(2), wrkv_ref[:, C:2 * C])
    w_lora = _bdot(jnp.tanh(_bdot(mixed(1), w1_ref[...])).astype(BF16), w2_ref[...])
    g_out[...] = _bdot(jax.nn.sigmoid(_bdot(mixed(5), g1_ref[...])).astype(BF16), g2_ref[...])
    r_out[...] = _bdot(mixed(0), wrkv_ref[:, 0:C])
    v_out[...] = _bdot(mixed(3), wrkv_ref[:, 2 * C:])

    lw_out[...] = -math.exp(-0.5) * jax.nn.sigmoid(w0_ref[...] + w_lora)
    a = jax.nn.sigmoid(a0_ref[...] + a_lora)
    kk = k * kk_ref[...]
    k_out[...] = k * (1.0 + (a - 1.0) * ka_ref[...])

    lane = lax.broadcasted_iota(jnp.int32, (tm, LANES), 1)
    first = lane < HEAD
    for s in range(C // LANES):
        sl = slice(s * LANES, (s + 1) * LANES)
        kks = kk[:, sl]
        kkn = kks * lax.rsqrt(jnp.maximum(_half_sums(kks * kks, first), 1e-24))
        an_out[:, sl] = -kkn
        bn_out[:, sl] = kkn * a[:, sl]


def _rwkv_scan_kernel(r_ref, lw_ref, k_ref, v_ref, a_ref, b_ref, g_ref, rk_ref, lg_ref, lb_ref, o_ref,
                      h_ref, qe_s, yl_s, mc_s, gc_s, bonus_s, gate_s, *, n_chunks, blocks_per_seq):
    t = pl.program_id(1)
    last = pl.num_programs(1) - 1
    carried = (qe_s, yl_s, mc_s, gc_s, bonus_s, gate_s)
    inputs = (r_ref, lw_ref, k_ref, v_ref, a_ref, b_ref, g_ref, rk_ref)

    def chain():
        return _rwkv_chain(lg_ref, lb_ref, o_ref, h_ref, *carried, n_chunks=n_chunks,
                           restart=(t - 1) % blocks_per_seq == 0)

    @pl.when(t == 0)
    def _():
        h_ref[...] = jnp.zeros_like(h_ref)
        _rwkv_chunks(*inputs, *carried, n_chunks=n_chunks, chain=iter(()))

    @pl.when((t > 0) & (t < last))
    def _():
        _rwkv_chunks(*inputs, *carried, n_chunks=n_chunks, chain=chain())

    @pl.when(t == last)
    def _():
        for _ in chain():
            pass


def _rwkv_chunks(r_ref, lw_ref, k_ref, v_ref, a_ref, b_ref, g_ref, rk_ref, qe_s, yl_s, mc_s, gc_s, bonus_s, gate_s,
                 *, n_chunks, chain):
    L = CHUNK

    row = lax.broadcasted_iota(jnp.int32, (2 * L, LANES), 0)
    col = lax.broadcasted_iota(jnp.int32, (2 * L, LANES), 1)
    block_diag = (row >= L) == (col >= L)
    rt = row & (L - 1)
    ct = col & (L - 1)
    strict = ct < rt
    incl = ct <= rt
    eye = (row == col).astype(F32)
    lane = lax.broadcasted_iota(jnp.int32, (L, LANES), 1)
    first = lane < HEAD
    tr = lax.broadcasted_iota(jnp.int32, (L, L), 0)
    tc = lax.broadcasted_iota(jnp.int32, (L, L), 1)
    tri = (tc <= tr).astype(BF16)

    def stack(z):
        return jnp.concatenate([jnp.where(first, z, 0.0), jnp.where(first, 0.0, z)], axis=0)

    def split3(z):
        hi = z.astype(BF16)
        r1 = z - hi.astype(F32)
        mid = r1.astype(BF16)
        return hi, mid, (r1 - mid.astype(F32)).astype(BF16)

    rk = rk_ref[...]

    chunks = range(n_chunks)
    sls = [pl.ds(c * L, L) for c in chunks]
    R = [r_ref[0, sl, :] for sl in sls]
    LW = [lw_ref[0, sl, :] for sl in sls]
    K = [k_ref[0, sl, :] for sl in sls]
    V = [v_ref[0, sl, :] for sl in sls]
    A = [a_ref[0, sl, :] for sl in sls]
    Bv = [b_ref[0, sl, :] for sl in sls]
    def stage(fn, ticks=1):
        out = [fn(c) for c in chunks]
        for _ in range(ticks):
            next(chain, None)
        return out

    cs = stage(lambda c: _bdot(tri, jnp.concatenate(split3(LW[c]), axis=1)), ticks=2)
    cum = [z[:, :LANES] + z[:, LANES:2 * LANES] + z[:, 2 * LANES:] for z in cs]
    cum_last = [z[L - 1:L, :] for z in cum]
    inv = [jnp.exp(-z) for z in cum]
    to_end = [jnp.exp(cum_last[c] - cum[c]) for c in chunks]
    Rt = [R[c] * jnp.exp(cum[c]) for c in chunks]
    At = [A[c] * jnp.exp(cum[c] - LW[c]) for c in chunks]
    Kt = [K[c] * inv[c] for c in chunks]
    Bt = [Bv[c] * inv[c] for c in chunks]
    AtS = [stack(z) for z in At]
    RtS = [stack(z) for z in Rt]
    BhT = [stack(Bv[c] * to_end[c]).T.astype(BF16) for c in chunks]
    KhT = [stack(K[c] * to_end[c]).T.astype(BF16) for c in chunks]
    kv = stage(lambda c: _bdot(KhT[c], stack(V[c]).astype(BF16)), ticks=2)

    p0 = stage(lambda c: _dot_nt(jnp.concatenate([AtS[c][:L], RtS[c][:L]], axis=0).astype(BF16),
                                 jnp.concatenate([Bt[c], Kt[c]], axis=0).astype(BF16)), ticks=2)
    p1 = stage(lambda c: _dot_nt(jnp.concatenate([AtS[c][L:], RtS[c][L:]], axis=0).astype(BF16),
                                 jnp.concatenate([Kt[c], Bt[c]], axis=0).astype(BF16)), ticks=2)
    top = [jnp.concatenate([p0[c][:L], p1[c][:L]], axis=0) for c in chunks]
    bot = [jnp.concatenate([p0[c][L:], p1[c][L:]], axis=0) for c in chunks]
    n_ab = [jnp.where(block_diag & strict, z, 0.0) for z in top]
    m_ak = [jnp.where(block_diag | ~strict, 0.0, z) for z in top]
    m_rb = [jnp.where(block_diag & incl, z, 0.0) for z in bot]
    m_rk = [jnp.where(block_diag | ~incl, 0.0, z) for z in bot]

    tinv = [eye + z for z in n_ab]
    xb = [z.astype(BF16) for z in n_ab]
    xb = [z.astype(BF16) for z in stage(lambda c: _bdot(xb[c], xb[c]))]
    for _ in range(4):
        xt = stage(lambda c: _bdot(xb[c], jnp.concatenate([xb[c], tinv[c].astype(BF16)], axis=1)))
        xb = [z[:, :LANES].astype(BF16) for z in xt]
        tinv = [tinv[c] + xt[c][:, LANES:] for c in chunks]
    last = stage(lambda c: _bdot(xb[c], tinv[c].astype(BF16)))
    tinv = [tinv[c] + last[c] for c in chunks]

    Vb = [z.astype(BF16) for z in V]
    zy = stage(lambda c: _bdot(jnp.concatenate([m_ak[c], m_rk[c]], axis=0).astype(BF16),
                               jnp.concatenate([Vb[c], Vb[c]], axis=0)))
    z_ak = [jnp.where(block_diag, z[:2 * L], 0.0) for z in zy]
    y_rk = [jnp.where(block_diag, z[2 * L:], 0.0) for z in zy]
    wub = [z.astype(BF16) for z in stage(
        lambda c: _bdot(tinv[c].astype(BF16), jnp.concatenate([AtS[c], z_ak[c]], axis=1).astype(BF16)))]
    qy = stage(lambda c: _bdot(m_rb[c].astype(BF16), wub[c]))
    q_eff = [(RtS[c] + qy[c][:, :LANES]).astype(BF16) for c in chunks]
    y_loc = [y_rk[c] + qy[c][:, LANES:] for c in chunks]
    mg = stage(lambda c: _bdot(BhT[c], wub[c]))
    m_c = [(eye * jnp.exp(cum_last[c]) + mg[c][:, :LANES]).astype(BF16) for c in chunks]
    g_c = [mg[c][:, LANES:] + kv[c] for c in chunks]

    for _ in chain:
        pass
    for c in chunks:
        qe_s[c] = q_eff[c]
        yl_s[c] = y_loc[c]
        mc_s[c] = m_c[c]
        gc_s[c] = g_c[c]
        bonus_s[c] = _half_sums(R[c] * K[c] * rk, first) * V[c]
        gate_s[c] = g_ref[0, sls[c], :]


def _rwkv_chain(lg_ref, lb_ref, o_ref, h_ref, qe_s, yl_s, mc_s, gc_s, bonus_s, gate_s, *, n_chunks, restart):
    L = CHUNK
    lane = lax.broadcasted_iota(jnp.int32, (L, LANES), 1)
    first = lane < HEAD
    lg = lg_ref[...]
    lb = lb_ref[...]
    h = jnp.where(restart, 0.0, h_ref[...])
    ys = []
    for c in range(n_chunks):
        both = _bdot(jnp.concatenate([qe_s[c], mc_s[c]], axis=0), h.astype(BF16))
        ys.append(both[:2 * L] + yl_s[c])
        h = both[2 * L:] + gc_s[c]
        yield
    h_ref[...] = h

    for c in range(n_chunks):
        y = ys[c][:L] + ys[c][L:]
        mu = _half_sums(y, first) * (1.0 / HEAD)
        d = y - mu
        var = _half_sums(d * d, first) * (1.0 / HEAD)
        yn = d * lax.rsqrt(var + RW_LNX_EPS) * lg + lb
        o_ref[0, pl.ds(c * L, L), :] = ((yn + bonus_s[c]) * gate_s[c]).astype(BF16)


def _rwkv_time_mix(x, B, T, g, mix, w_rkv, w0, w1, w2, a0, a1, a2, g1, g2, k_k, k_a, r_k, lnx_g, lnx_b,
                   *, tm=512, tt=1024):
    M, C = x.shape
    row = pl.BlockSpec((tm, C), lambda i: (i, 0))
    prev = pl.BlockSpec((SUBLANES, C), lambda i: (jnp.maximum(i * (tm // SUBLANES) - 1, 0), 0))
    vec = lambda a: a.reshape(1, C)
    outs = pl.pallas_call(
        functools.partial(_rwkv_prep_kernel, tiles_per_seq=T // tm),
        grid=(M // tm,),
        in_specs=[row, prev, _resident((1, C)), _resident(mix.shape), _resident(w_rkv.shape),
                  _resident(w1.shape), _resident(w2.shape), _resident(a1.shape), _resident(a2.shape),
                  _resident(g1.shape), _resident(g2.shape)] + [_resident((1, C))] * 4,
        out_specs=[row] * 7,
        out_shape=[jax.ShapeDtypeStruct((M, C), F32)] * 7,
        compiler_params=_params("parallel"),
        name="rwkv_prep",
    )(x, x, vec(g), mix, w_rkv, w1, w2, a1, a2, g1, g2, vec(w0), vec(a0), vec(k_k), vec(k_a))
    seq = [z.reshape(B, T, C) for z in outs]
    nt = T // tt
    n_chunks = tt // CHUNK
    def block(t):
        return t // nt, t % nt

    slab_in = pl.BlockSpec((1, tt, LANES), lambda s, t: (*block(jnp.minimum(t, B * nt - 1)), s))
    slab_out = pl.BlockSpec((1, tt, LANES), lambda s, t: (*block(jnp.maximum(t - 1, 0)), s))
    pvec = pl.BlockSpec((1, LANES), lambda s, t: (0, s))
    stacked = (n_chunks, 2 * CHUNK, LANES)
    y = pl.pallas_call(
        functools.partial(_rwkv_scan_kernel, n_chunks=n_chunks, blocks_per_seq=nt),
        grid=(C // LANES, B * nt + 1),
        in_specs=[slab_in] * 7 + [pvec] * 3,
        out_specs=slab_out,
        out_shape=jax.ShapeDtypeStruct((B, T, C), BF16),
        scratch_shapes=[pltpu.VMEM((LANES, LANES), F32),
                        pltpu.VMEM(stacked, BF16), pltpu.VMEM(stacked, F32),
                        pltpu.VMEM(stacked, BF16), pltpu.VMEM(stacked, F32),
                        pltpu.VMEM((n_chunks, CHUNK, LANES), F32), pltpu.VMEM((n_chunks, CHUNK, LANES), F32)],
        compiler_params=_params("parallel", "arbitrary"),
        name="rwkv_scan",
    )(*seq, vec(r_k), vec(lnx_g), vec(lnx_b))
    return y.reshape(M, C)


def _pool_ffn_kernel(x_ref, xh_ref, gm_ref, pw_ref, sc_ref, g_ref, win_ref, wout_ref, o_ref,
                     *, hidden, tf, tiles_per_seq):
    ti = pl.program_id(0) % tiles_per_seq
    tm, C = x_ref.shape
    group = C // len(POOL_WINDOWS)
    gm = gm_ref[...]
    x = x_ref[...]
    h = _rms(x, gm, NORM_EPS)
    halo = jnp.where(ti == 0, 0.0, _rms(xh_ref[...], gm, NORM_EPS))
    he = jnp.concatenate([halo, h], axis=0)
    t = ti * tm + lax.broadcasted_iota(jnp.int32, (tm, 1), 0)
    ys = []
    for gi, win in enumerate(POOL_WINDOWS):
        sl = slice(gi * group, (gi + 1) * group)
        s = he[:, sl]
        shift = 1
        while shift < win:
            s = s + pltpu.roll(s, shift, 0)
            shift *= 2
        cnt = jnp.minimum(t + 1, win).astype(F32)
        delta = (s[POOL_HALO:] / cnt - h[:, sl]).astype(BF16)
        ys.append(_bdot(delta, pw_ref[gi]))
    x1 = x + jnp.concatenate(ys, axis=1) * sc_ref[...]
    _ffn_residual(x1, g_ref, win_ref, wout_ref, o_ref, hidden, tf)


def _pool_ffn(x, T, g_mix, pool_w, scale, g, w_in, w_out, *, tm=512, tf=256):
    M, C = x.shape
    hidden = w_out.shape[0]
    blocks = tm // POOL_HALO
    row = pl.BlockSpec((tm, C), lambda i: (i, 0))
    return pl.pallas_call(
        functools.partial(_pool_ffn_kernel, hidden=hidden, tf=tf, tiles_per_seq=T // tm),
        grid=(M // tm,),
        in_specs=[row, pl.BlockSpec((POOL_HALO, C), lambda i: (jnp.maximum(i * blocks - 1, 0), 0)),
                  _resident((1, C)), _resident(pool_w.shape), _resident((1, C)), _resident((1, C)),
                  _resident((C, 2 * hidden)), _resident((hidden, C))],
        out_specs=row,
        out_shape=jax.ShapeDtypeStruct((M, C), F32),
        compiler_params=_params("parallel"),
        name="pool_ffn",
    )(x, x, g_mix.reshape(1, C), pool_w, scale.reshape(1, C), g.reshape(1, C), w_in, w_out)


def kernel(x, mix_norm, ffn_norm, ffn_w_in, ffn_w_out, da_wqkv, da_wo, da_q_gain, da_k_gain, da_lambda, da_subln, rw_mix, rw_wrkv, rw_wo, rw_w0, rw_w1, rw_w2, rw_a0, rw_a1, rw_a2, rw_g1, rw_g2, rw_kk, rw_ka, rw_rk, rw_lnx_g, rw_lnx_b, pool_w, pool_scale):
    B, T, C = x.shape
    depth = mix_norm.shape[0]
    bf = lambda w: w.astype(BF16)
    xs = x.reshape(B * T, C)
    ia = ir = ip = 0
    for layer in range(depth):
        kind = layer % N_MIXERS
        ffn = (ffn_norm[layer], bf(ffn_w_in[layer]), bf(ffn_w_out[layer]))
        if kind == 0:
            lambda_init = 0.8 - 0.6 * math.exp(-0.3 * layer)
            y = _diff_attention(xs, B, T, mix_norm[layer], bf(da_wqkv[ia]), da_q_gain[ia], da_k_gain[ia],
                                da_lambda[ia], da_subln[ia], lambda_init)
            xs = _mix_ffn(xs, y, bf(da_wo[ia]), *ffn)
            ia += 1
        elif kind == 1:
            y = _rwkv_time_mix(xs, B, T, mix_norm[layer], rw_mix[ir], bf(rw_wrkv[ir]), rw_w0[ir], bf(rw_w1[ir]),
                               bf(rw_w2[ir]), rw_a0[ir], bf(rw_a1[ir]), bf(rw_a2[ir]), bf(rw_g1[ir]),
                               bf(rw_g2[ir]), rw_kk[ir], rw_ka[ir], rw_rk[ir], rw_lnx_g[ir], rw_lnx_b[ir])
            xs = _mix_ffn(xs, y, bf(rw_wo[ir]), *ffn)
            ir += 1
        else:
            xs = _pool_ffn(xs, T, mix_norm[layer], bf(pool_w[ip]), pool_scale[ip], *ffn)
            ip += 1
    return xs.reshape(B, T, C)
```

```python
import functools
import math

import jax
import jax.numpy as jnp
from jax import lax
from jax.experimental import pallas as pl
from jax.experimental.pallas import tpu as pltpu

F32 = jnp.float32
BF16 = jnp.bfloat16

LANES = 128
SUBLANES = 8
VMEM_LIMIT_BYTES = 48 * 1024 * 1024

CHUNK = 64
HEAD = 64
N_MIXERS = 3
NORM_EPS = 1e-6
QK_EPS = 1e-6
SUBLN_EPS = 1e-5
ROPE_THETA = 10000.0
RW_LNX_EPS = 64e-5
POOL_WINDOWS = (2, 4, 8, 16)
POOL_HALO = 16
NEG = -1e30


def _rms(z, gain, eps):
    return z * lax.rsqrt(jnp.mean(z * z, axis=-1, keepdims=True) + eps) * gain


def _bdot(a, b):
    return jnp.dot(a, b, preferred_element_type=F32)


def _dot_nt(a, b):
    return lax.dot_general(a, b, (((1,), (1,)), ((), ())), preferred_element_type=F32)


def _dot_tn(a, b):
    return lax.dot_general(a, b, (((0,), (0,)), ((), ())), preferred_element_type=F32)


def _half_sums(z, first):
    lo = jnp.sum(jnp.where(first, z, 0.0), axis=-1, keepdims=True)
    hi = jnp.sum(jnp.where(first, 0.0, z), axis=-1, keepdims=True)
    return jnp.where(first, lo, hi)


def _params(*sem):
    return pltpu.CompilerParams(dimension_semantics=sem, vmem_limit_bytes=VMEM_LIMIT_BYTES)


def _resident(shape):
    return pl.BlockSpec(shape, lambda *_: (0,) * len(shape), pipeline_mode=pl.Buffered(1))


def _ffn_residual(x1, g_ref, win_ref, wout_ref, o_ref, hidden, tf):
    hn = _rms(x1, g_ref[...], NORM_EPS).astype(BF16)
    acc = x1
    for f in range(hidden // tf):
        gate = _bdot(hn, win_ref[:, f * tf:(f + 1) * tf])
        up = _bdot(hn, win_ref[:, hidden + f * tf:hidden + (f + 1) * tf])
        act = (gate * jax.nn.sigmoid(gate) * up).astype(BF16)
        acc = acc + _bdot(act, wout_ref[f * tf:(f + 1) * tf, :])
    o_ref[...] = acc


def _mix_ffn_kernel(x_ref, y_ref, wm_ref, g_ref, win_ref, wout_ref, o_ref, *, hidden, tf):
    x1 = x_ref[...] + _bdot(y_ref[...], wm_ref[...])
    _ffn_residual(x1, g_ref, win_ref, wout_ref, o_ref, hidden, tf)


def _mix_ffn(x, y, w_mix, g, w_in, w_out, *, tm=1024, tf=256):
    M, C = x.shape
    hidden = w_out.shape[0]
    row = pl.BlockSpec((tm, C), lambda i: (i, 0))
    return pl.pallas_call(
        functools.partial(_mix_ffn_kernel, hidden=hidden, tf=tf),
        grid=(M // tm,),
        in_specs=[row, row, _resident((C, C)), _resident((1, C)), _resident((C, 2 * hidden)),
                  _resident((hidden, C))],
        out_specs=row,
        out_shape=jax.ShapeDtypeStruct((M, C), F32),
        compiler_params=_params("parallel"),
        name="mix_ffn",
    )(x, y, w_mix, g.reshape(1, C), w_in, w_out)


def _qkv_kernel(x_ref, g_ref, wqt_ref, wk_ref, wvt_ref, qgt_ref, kg_ref, cos_ref, sin_ref, cost_ref, sint_ref,
                qt_ref, k_ref, vt_ref):
    tm, C = x_ref.shape
    hn = _rms(x_ref[...], g_ref[...], NORM_EPS).astype(BF16)

    k = _bdot(hn, wk_ref[...])
    vt_ref[0, 0] = _dot_nt(wvt_ref[...], hn).astype(BF16)

    q3 = _dot_nt(wqt_ref[...], hn).reshape(C // HEAD, HEAD, tm)
    ms = jnp.mean(q3 * q3, axis=1, keepdims=True)
    qn = q3 * lax.rsqrt(ms + QK_EPS) * qgt_ref[...][None]
    partner = jnp.concatenate([qn[:, HEAD // 2:], qn[:, :HEAD // 2]], axis=1)
    q3 = qn * cost_ref[...][None] + partner * sint_ref[...][None]
    qt_ref[0, 0] = q3.reshape(C, tm).astype(BF16)

    cos = cos_ref[...]
    sin = sin_ref[...]
    lane = lax.broadcasted_iota(jnp.int32, cos.shape, 1)
    first = lane < HEAD
    low_half = (lane & (HEAD - 1)) < HEAD // 2
    for s in range(C // LANES):
        sl = slice(s * LANES, (s + 1) * LANES)
        z = k[:, sl]
        zn = z * lax.rsqrt(_half_sums(z * z, first) * (1.0 / HEAD) + QK_EPS) * kg_ref[...]
        partner = jnp.where(low_half, pltpu.roll(zn, LANES - HEAD // 2, 1), pltpu.roll(zn, HEAD // 2, 1))
        k_ref[:, sl] = (zn * cos + partner * sin).astype(BF16)


def _attn_kernel(qt_ref, k_ref, vt_ref, lam_ref, sg_ref, o_ref, s_refs, m_refs, l_refs, acc_refs,
                 *, tq, lambda_init):
    i = pl.program_id(2)
    feat = lax.broadcasted_iota(jnp.int32, (LANES, tq), 0)

    def stacked_q(h):
        qt = qt_ref[0, 0, h * LANES:(h + 1) * LANES, :]
        zero = jnp.zeros_like(qt)
        return jnp.concatenate([jnp.where(feat < HEAD, qt, zero), jnp.where(feat < HEAD, zero, qt)], axis=1)

    qst = [stacked_q(0), stacked_q(1)]

    def scores(h, j, diagonal=False):
        start = pl.multiple_of(j * tq, tq)
        s = _bdot(k_ref[0, pl.ds(start, tq), h * LANES:(h + 1) * LANES], qst[h])
        if diagonal:
            key = lax.broadcasted_iota(jnp.int32, s.shape, 0)
            qry = lax.broadcasted_iota(jnp.int32, s.shape, 1)
            qry = jnp.where(qry >= tq, qry - tq, qry)
            s = jnp.where((key // CHUNK) <= (qry // CHUNK), s, NEG)
        s_refs[h] = s

    def absorb(h, j):
        m = m_refs[h]
        m_new = jnp.maximum(m, jnp.max(s_refs[h], axis=0, keepdims=True))
        alpha = jnp.exp2(m - m_new)
        p = jnp.exp2(s_refs[h] - m_new)
        m_refs[h] = m_new
        l_refs[h] = alpha * l_refs[h] + jnp.sum(p, axis=0, keepdims=True)
        vt = vt_ref[0, j, h * LANES:(h + 1) * LANES, :]
        acc_refs[h] = alpha * acc_refs[h] + _bdot(vt, p.astype(BF16))

    def finish(h):
        o = acc_refs[h] / l_refs[h]
        lv = lam_ref[...]
        lam = (jnp.exp(jnp.sum(lv[0:1] * lv[1:2], axis=-1, keepdims=True))
               - jnp.exp(jnp.sum(lv[2:3] * lv[3:4], axis=-1, keepdims=True)) + lambda_init)
        d = (o[:, :tq] - lam * o[:, tq:]).T
        o_ref[0, :, h * LANES:(h + 1) * LANES] = (_rms(d, sg_ref[...], SUBLN_EPS)
                                                  * (1.0 - lambda_init)).astype(BF16)

    m_refs[...] = jnp.full(m_refs.shape, NEG, F32)
    l_refs[...] = jnp.zeros(l_refs.shape, F32)
    acc_refs[...] = jnp.zeros(acc_refs.shape, F32)
    scores(0, i, diagonal=True)
    scores(1, i, diagonal=True)
    absorb(0, i)

    def step(n, _):
        scores(0, n)
        absorb(1, jnp.where(n == 0, i, n - 1))
        scores(1, n)
        absorb(0, n)
        return 0

    lax.fori_loop(0, i, step, 0)
    absorb(1, jnp.where(i == 0, i, i - 1))
    finish(0)
    finish(1)


def _rope_tables(T):
    half = HEAD // 2
    inv = 1.0 / (ROPE_THETA ** (jnp.arange(0, HEAD, 2, dtype=F32) / HEAD))
    ang = jnp.arange(T, dtype=F32)[:, None] * inv[None, :]
    cos, sin = jnp.cos(ang), jnp.sin(ang)
    cos_h = jnp.concatenate([cos, cos], axis=1)
    sin_h = jnp.concatenate([-sin, sin], axis=1)
    reps = LANES // HEAD
    return jnp.tile(cos_h, (1, reps)), jnp.tile(sin_h, (1, reps)), cos_h.T, sin_h.T


def _diff_attention(x, B, T, g, w_qkv, q_gain, k_gain, lam_vec, subln_g, lambda_init, *, tq=512):
    M, C = x.shape
    H = C // LANES
    nt = T // tq
    cos_t, sin_t, cos_tt, sin_tt = _rope_tables(T)
    qgt = jnp.broadcast_to((q_gain * (HEAD ** -0.5 * math.log2(math.e)))[:, None], (HEAD, tq))
    kg = jnp.tile(k_gain, LANES // HEAD).reshape(1, LANES)
    wqt, wk, wvt = w_qkv[:, :C].T, w_qkv[:, C:2 * C], w_qkv[:, 2 * C:].T
    row = pl.BlockSpec((tq, C), lambda i: (i, 0))
    table = pl.BlockSpec((tq, LANES), lambda i: (i % nt, 0))
    table_t = pl.BlockSpec((HEAD, tq), lambda i: (0, i % nt))
    transposed = pl.BlockSpec((1, 1, C, tq), lambda i: (i // nt, i % nt, 0, 0))
    qt, k, vt = pl.pallas_call(
        _qkv_kernel,
        grid=(M // tq,),
        in_specs=[row, _resident((1, C)), _resident((C, C)), _resident((C, C)), _resident((C, C)),
                  _resident((HEAD, tq)), _resident((1, LANES)), table, table, table_t, table_t],
        out_specs=[transposed, row, transposed],
        out_shape=[jax.ShapeDtypeStruct((B, nt, C, tq), BF16), jax.ShapeDtypeStruct((M, C), BF16),
                   jax.ShapeDtypeStruct((B, nt, C, tq), BF16)],
        compiler_params=_params("parallel"),
        name="attn_qkv",
    )(x, g.reshape(1, C), wqt, wk, wvt, qgt, kg, cos_t, sin_t, cos_tt, sin_tt)
    o = pl.pallas_call(
        functools.partial(_attn_kernel, tq=tq, lambda_init=lambda_init),
        grid=(B, H // 2, nt),
        in_specs=[pl.BlockSpec((1, 1, 2 * LANES, tq), lambda b, h, i: (b, i, h, 0)),
                  pl.BlockSpec((1, T, 2 * LANES), lambda b, h, i: (b, 0, h)),
                  pl.BlockSpec((1, nt, 2 * LANES, tq), lambda b, h, i: (b, 0, h, 0)),
                  _resident((4, HEAD)), _resident((1, LANES))],
        out_specs=pl.BlockSpec((1, tq, 2 * LANES), lambda b, h, i: (b, i, h)),
        out_shape=jax.ShapeDtypeStruct((B, T, C), BF16),
        scratch_shapes=[pltpu.VMEM((2, tq, 2 * tq), F32), pltpu.VMEM((2, 1, 2 * tq), F32),
                        pltpu.VMEM((2, 1, 2 * tq), F32), pltpu.VMEM((2, LANES, 2 * tq), F32)],
        compiler_params=_params("parallel", "parallel", "arbitrary"),
        name="attn_core",
    )(qt, k.reshape(B, T, C), vt, lam_vec, subln_g.reshape(1, LANES))
    return o.reshape(M, C)


def _rwkv_prep_kernel(x_ref, xp_ref, g_ref, mix_ref, wrkv_ref, w1_ref, w2_ref, a1_ref, a2_ref, g1_ref, g2_ref,
                      w0_ref, a0_ref, kk_ref, ka_ref,
                      r_out, lw_out, k_out, v_out, an_out, bn_out, g_out, *, tiles_per_seq):
    i = pl.program_id(0)
    tm, C = x_ref.shape
    g = g_ref[...]
    h = _rms(x_ref[...], g, NORM_EPS)
    h_last = _rms(xp_ref[SUBLANES - 1:SUBLANES, :], g, NORM_EPS)
    h_last = jnp.where(i % tiles_per_seq == 0, 0.0, h_last)
    row = lax.broadcasted_iota(jnp.int32, h.shape, 0)
    dx = jnp.where(row == 0, h_last, pltpu.roll(h, 1, 0)) - h
    mix = mix_ref[...]

    def mixed(n):
        return (h + dx * mix[n:n + 1]).astype(BF16)

    a_lora = _bdot(_bdot(mixed(4), a1_ref[...]).astype(BF16), a2_ref[...])
    k = _bdot(mixed(2), wrkv_ref[:, C:2 * C])
    w_lora = _bdot(jnp.tanh(_bdot(mixed(1), w1_ref[...])).astype(BF16), w2_ref[...])
    g_out[...] = _bdot(jax.nn.sigmoid(_bdot(mixed(5), g1_ref[...])).astype(BF16), g2_ref[...])
    r_out[...] = _bdot(mixed(0), wrkv_ref[:, 0:C])
    v_out[...] = _bdot(mixed(3), wrkv_ref[:, 2 * C:])

    lw_out[...] = -math.exp(-0.5) * jax.nn.sigmoid(w0_ref[...] + w_lora)
    a = jax.nn.sigmoid(a0_ref[...] + a_lora)
    kk = k * kk_ref[...]
    k_out[...] = k * (1.0 + (a - 1.0) * ka_ref[...])

    lane = lax.broadcasted_iota(jnp.int32, (tm, LANES), 1)
    first = lane < HEAD
    for s in range(C // LANES):
        sl = slice(s * LANES, (s + 1) * LANES)
        kks = kk[:, sl]
        kkn = kks * lax.rsqrt(jnp.maximum(_half_sums(kks * kks, first), 1e-24))
        an_out[:, sl] = -kkn
        bn_out[:, sl] = kkn * a[:, sl]


def _rwkv_scan_kernel(r_ref, lw_ref, k_ref, v_ref, a_ref, b_ref, g_ref, rk_ref, lg_ref, lb_ref, o_ref,
                      h_ref, qe_s, yl_s, mc_s, gc_s, bonus_s, gate_s, *, n_chunks, blocks_per_seq):
    t = pl.program_id(1)
    last = pl.num_programs(1) - 1
    carried = (qe_s, yl_s, mc_s, gc_s, bonus_s, gate_s)
    inputs = (r_ref, lw_ref, k_ref, v_ref, a_ref, b_ref, g_ref, rk_ref)

    def chain():
        return _rwkv_chain(lg_ref, lb_ref, o_ref, h_ref, *carried, n_chunks=n_chunks,
                           restart=(t - 1) % blocks_per_seq == 0)

    @pl.when(t == 0)
    def _():
        h_ref[...] = jnp.zeros_like(h_ref)
        _rwkv_chunks(*inputs, *carried, n_chunks=n_chunks, chain=iter(()))

    @pl.when((t > 0) & (t < last))
    def _():
        _rwkv_chunks(*inputs, *carried, n_chunks=n_chunks, chain=chain())

    @pl.when(t == last)
    def _():
        for _ in chain():
            pass


def _rwkv_chunks(r_ref, lw_ref, k_ref, v_ref, a_ref, b_ref, g_ref, rk_ref, qe_s, yl_s, mc_s, gc_s, bonus_s, gate_s,
                 *, n_chunks, chain):
    L = CHUNK

    row = lax.broadcasted_iota(jnp.int32, (2 * L, LANES), 0)
    col = lax.broadcasted_iota(jnp.int32, (2 * L, LANES), 1)
    block_diag = (row >= L) == (col >= L)
    rt = row & (L - 1)
    ct = col & (L - 1)
    strict = ct < rt
    incl = ct <= rt
    eye = (row == col).astype(F32)
    lane = lax.broadcasted_iota(jnp.int32, (L, LANES), 1)
    first = lane < HEAD
    tr = lax.broadcasted_iota(jnp.int32, (L, L), 0)
    tc = lax.broadcasted_iota(jnp.int32, (L, L), 1)
    tri = (tc <= tr).astype(BF16)

    def stack(z):
        return jnp.concatenate([jnp.where(first, z, 0.0), jnp.where(first, 0.0, z)], axis=0)

    def split3(z):
        hi = z.astype(BF16)
        r1 = z - hi.astype(F32)
        mid = r1.astype(BF16)
        return hi, mid, (r1 - mid.astype(F32)).astype(BF16)

    rk = rk_ref[...]

    chunks = range(n_chunks)
    sls = [pl.ds(c * L, L) for c in chunks]
    R = [r_ref[0, sl, :] for sl in sls]
    LW = [lw_ref[0, sl, :] for sl in sls]
    K = [k_ref[0, sl, :] for sl in sls]
    V = [v_ref[0, sl, :] for sl in sls]
    A = [a_ref[0, sl, :] for sl in sls]
    Bv = [b_ref[0, sl, :] for sl in sls]
    def stage(fn, ticks=1):
        out = [fn(c) for c in chunks]
        for _ in range(ticks):
            next(chain, None)
        return out

    cs = stage(lambda c: _bdot(tri, jnp.concatenate(split3(LW[c]), axis=1)), ticks=2)
    cum = [z[:, :LANES] + z[:, LANES:2 * LANES] + z[:, 2 * LANES:] for z in cs]
    cum_last = [z[L - 1:L, :] for z in cum]
    inv = [jnp.exp(-z) for z in cum]
    to_end = [jnp.exp(cum_last[c] - cum[c]) for c in chunks]
    Rt = [R[c] * jnp.exp(cum[c]) for c in chunks]
    At = [A[c] * jnp.exp(cum[c] - LW[c]) for c in chunks]
    Kt = [K[c] * inv[c] for c in chunks]
    Bt = [Bv[c] * inv[c] for c in chunks]
    AtS = [stack(z) for z in At]
    RtS = [stack(z) for z in Rt]
    BhT = [stack(Bv[c] * to_end[c]).T.astype(BF16) for c in chunks]
    KhT = [stack(K[c] * to_end[c]).T.astype(BF16) for c in chunks]
    kv = stage(lambda c: _bdot(KhT[c], stack(V[c]).astype(BF16)), ticks=2)

    p0 = stage(lambda c: _dot_nt(jnp.concatenate([AtS[c][:L], RtS[c][:L]], axis=0).astype(BF16),
                                 jnp.concatenate([Bt[c], Kt[c]], axis=0).astype(BF16)), ticks=2)
    p1 = stage(lambda c: _dot_nt(jnp.concatenate([AtS[c][L:], RtS[c][L:]], axis=0).astype(BF16),
                                 jnp.concatenate([Kt[c], Bt[c]], axis=0).astype(BF16)), ticks=2)
    top = [jnp.concatenate([p0[c][:L], p1[c][:L]], axis=0) for c in chunks]
    bot = [jnp.concatenate([p0[c][L:], p1[c][L:]], axis=0) for c in chunks]
    n_ab = [jnp.where(block_diag & strict, z, 0.0) for z in top]
    m_ak = [jnp.where(block_diag | ~strict, 0.0, z) for z in top]
    m_rb = [jnp.where(block_diag & incl, z, 0.0) for z in bot]
    m_rk = [jnp.where(block_diag | ~incl, 0.0, z) for z in bot]

    tinv = [eye + z for z in n_ab]
    xb = [z.astype(BF16) for z in n_ab]
    xb = [z.astype(BF16) for z in stage(lambda c: _bdot(xb[c], xb[c]))]
    for _ in range(4):
        xt = stage(lambda c: _bdot(xb[c], jnp.concatenate([xb[c], tinv[c].astype(BF16)], axis=1)))
        xb = [z[:, :LANES].astype(BF16) for z in xt]
        tinv = [tinv[c] + xt[c][:, LANES:] for c in chunks]
    last = stage(lambda c: _bdot(xb[c], tinv[c].astype(BF16)))
    tinv = [tinv[c] + last[c] for c in chunks]

    Vb = [z.astype(BF16) for z in V]
    zy = stage(lambda c: _bdot(jnp.concatenate([m_ak[c], m_rk[c]], axis=0).astype(BF16),
                               jnp.concatenate([Vb[c], Vb[c]], axis=0)))
    z_ak = [jnp.where(block_diag, z[:2 * L], 0.0) for z in zy]
    y_rk = [jnp.where(block_diag, z[2 * L:], 0.0) for z in zy]
    wub = [z.astype(BF16) for z in stage(
        lambda c: _bdot(tinv[c].astype(BF16), jnp.concatenate([AtS[c], z_ak[c]], axis=1).astype(BF16)))]
    qy = stage(lambda c: _bdot(m_rb[c].astype(BF16), wub[c]))
    q_eff = [(RtS[c] + qy[c][:, :LANES]).astype(BF16) for c in chunks]
    y_loc = [y_rk[c] + qy[c][:, LANES:] for c in chunks]
    mg = stage(lambda c: _bdot(BhT[c], wub[c]))
    m_c = [(eye * jnp.exp(cum_last[c]) + mg[c][:, :LANES]).astype(BF16) for c in chunks]
    g_c = [mg[c][:, LANES:] + kv[c] for c in chunks]

    for _ in chain:
        pass
    for c in chunks:
        qe_s[c] = q_eff[c]
        yl_s[c] = y_loc[c]
        mc_s[c] = m_c[c]
        gc_s[c] = g_c[c]
        bonus_s[c] = _half_sums(R[c] * K[c] * rk, first) * V[c]
        gate_s[c] = g_ref[0, sls[c], :]


def _rwkv_chain(lg_ref, lb_ref, o_ref, h_ref, qe_s, yl_s, mc_s, gc_s, bonus_s, gate_s, *, n_chunks, restart):
    L = CHUNK
    lane = lax.broadcasted_iota(jnp.int32, (L, LANES), 1)
    first = lane < HEAD
    lg = lg_ref[...]
    lb = lb_ref[...]
    h = jnp.where(restart, 0.0, h_ref[...])
    ys = []
    for c in range(n_chunks):
        both = _bdot(jnp.concatenate([qe_s[c], mc_s[c]], axis=0), h.astype(BF16))
        ys.append(both[:2 * L] + yl_s[c])
        h = both[2 * L:] + gc_s[c]
        yield
    h_ref[...] = h

    for c in range(n_chunks):
        y = ys[c][:L] + ys[c][L:]
        mu = _half_sums(y, first) * (1.0 / HEAD)
        d = y - mu
        var = _half_sums(d * d, first) * (1.0 / HEAD)
        yn = d * lax.rsqrt(var + RW_LNX_EPS) * lg + lb
        o_ref[0, pl.ds(c * L, L), :] = ((yn + bonus_s[c]) * gate_s[c]).astype(BF16)


def _rwkv_time_mix(x, B, T, g, mix, w_rkv, w0, w1, w2, a0, a1, a2, g1, g2, k_k, k_a, r_k, lnx_g, lnx_b,
                   *, tm=512, tt=1024):
    M, C = x.shape
    row = pl.BlockSpec((tm, C), lambda i: (i, 0))
    prev = pl.BlockSpec((SUBLANES, C), lambda i: (jnp.maximum(i * (tm // SUBLANES) - 1, 0), 0))
    vec = lambda a: a.reshape(1, C)
    outs = pl.pallas_call(
        functools.partial(_rwkv_prep_kernel, tiles_per_seq=T // tm),
        grid=(M // tm,),
        in_specs=[row, prev, _resident((1, C)), _resident(mix.shape), _resident(w_rkv.shape),
                  _resident(w1.shape), _resident(w2.shape), _resident(a1.shape), _resident(a2.shape),
                  _resident(g1.shape), _resident(g2.shape)] + [_resident((1, C))] * 4,
        out_specs=[row] * 7,
        out_shape=[jax.ShapeDtypeStruct((M, C), F32)] * 7,
        compiler_params=_params("parallel"),
        name="rwkv_prep",
    )(x, x, vec(g), mix, w_rkv, w1, w2, a1, a2, g1, g2, vec(w0), vec(a0), vec(k_k), vec(k_a))
    seq = [z.reshape(B, T, C) for z in outs]
    nt = T // tt
    n_chunks = tt // CHUNK
    def block(t):
        return t // nt, t % nt

    slab_in = pl.BlockSpec((1, tt, LANES), lambda s, t: (*block(jnp.minimum(t, B * nt - 1)), s))
    slab_out = pl.BlockSpec((1, tt, LANES), lambda s, t: (*block(jnp.maximum(t - 1, 0)), s))
    pvec = pl.BlockSpec((1, LANES), lambda s, t: (0, s))
    stacked = (n_chunks, 2 * CHUNK, LANES)
    y = pl.pallas_call(
        functools.partial(_rwkv_scan_kernel, n_chunks=n_chunks, blocks_per_seq=nt),
        grid=(C // LANES, B * nt + 1),
        in_specs=[slab_in] * 7 + [pvec] * 3,
        out_specs=slab_out,
        out_shape=jax.ShapeDtypeStruct((B, T, C), BF16),
        scratch_shapes=[pltpu.VMEM((LANES, LANES), F32),
                        pltpu.VMEM(stacked, BF16), pltpu.VMEM(stacked, F32),
                        pltpu.VMEM(stacked, BF16), pltpu.VMEM(stacked, F32),
                        pltpu.VMEM((n_chunks, CHUNK, LANES), F32), pltpu.VMEM((n_chunks, CHUNK, LANES), F32)],
        compiler_params=_params("parallel", "arbitrary"),
        name="rwkv_scan",
    )(*seq, vec(r_k), vec(lnx_g), vec(lnx_b))
    return y.reshape(M, C)


def _pool_ffn_kernel(x_ref, xh_ref, gm_ref, pw_ref, sc_ref, g_ref, win_ref, wout_ref, o_ref,
                     *, hidden, tf, tiles_per_seq):
    ti = pl.program_id(0) % tiles_per_seq
    tm, C = x_ref.shape
    group = C // len(POOL_WINDOWS)
    gm = gm_ref[...]
    x = x_ref[...]
    h = _rms(x, gm, NORM_EPS)
    halo = jnp.where(ti == 0, 0.0, _rms(xh_ref[...], gm, NORM_EPS))
    he = jnp.concatenate([halo, h], axis=0)
    t = ti * tm + lax.broadcasted_iota(jnp.int32, (tm, 1), 0)
    ys = []
    for gi, win in enumerate(POOL_WINDOWS):
        sl = slice(gi * group, (gi + 1) * group)
        s = he[:, sl]
        shift = 1
        while shift < win:
            s = s + pltpu.roll(s, shift, 0)
            shift *= 2
        cnt = jnp.minimum(t + 1, win).astype(F32)
        delta = (s[POOL_HALO:] / cnt - h[:, sl]).astype(BF16)
        ys.append(_bdot(delta, pw_ref[gi]))
    x1 = x + jnp.concatenate(ys, axis=1) * sc_ref[...]
    _ffn_residual(x1, g_ref, win_ref, wout_ref, o_ref, hidden, tf)


def _pool_ffn(x, T, g_mix, pool_w, scale, g, w_in, w_out, *, tm=512, tf=256):
    M, C = x.shape
    hidden = w_out.shape[0]
    blocks = tm // POOL_HALO
    row = pl.BlockSpec((tm, C), lambda i: (i, 0))
    return pl.pallas_call(
        functools.partial(_pool_ffn_kernel, hidden=hidden, tf=tf, tiles_per_seq=T // tm),
        grid=(M // tm,),
        in_specs=[row, pl.BlockSpec((POOL_HALO, C), lambda i: (jnp.maximum(i * blocks - 1, 0), 0)),
                  _resident((1, C)), _resident(pool_w.shape), _resident((1, C)), _resident((1, C)),
                  _resident((C, 2 * hidden)), _resident((hidden, C))],
        out_specs=row,
        out_shape=jax.ShapeDtypeStruct((M, C), F32),
        compiler_params=_params("parallel"),
        name="pool_ffn",
    )(x, x, g_mix.reshape(1, C), pool_w, scale.reshape(1, C), g.reshape(1, C), w_in, w_out)


def kernel(x, mix_norm, ffn_norm, ffn_w_in, ffn_w_out, da_wqkv, da_wo, da_q_gain, da_k_gain, da_lambda, da_subln, rw_mix, rw_wrkv, rw_wo, rw_w0, rw_w1, rw_w2, rw_a0, rw_a1, rw_a2, rw_g1, rw_g2, rw_kk, rw_ka, rw_rk, rw_lnx_g, rw_lnx_b, pool_w, pool_scale):
    B, T, C = x.shape
    depth = mix_norm.shape[0]
    bf = lambda w: w.astype(BF16)
    xs = x.reshape(B * T, C)
    ia = ir = ip = 0
    for layer in range(depth):
        kind = layer % N_MIXERS
        ffn = (ffn_norm[layer], bf(ffn_w_in[layer]), bf(ffn_w_out[layer]))
        if kind == 0:
            lambda_init = 0.8 - 0.6 * math.exp(-0.3 * layer)
            y = _diff_attention(xs, B, T, mix_norm[layer], bf(da_wqkv[ia]), da_q_gain[ia], da_k_gain[ia],
                                da_lambda[ia], da_subln[ia], lambda_init)
            xs = _mix_ffn(xs, y, bf(da_wo[ia]), *ffn)
            ia += 1
        elif kind == 1:
            y = _rwkv_time_mix(xs, B, T, mix_norm[layer], rw_mix[ir], bf(rw_wrkv[ir]), rw_w0[ir], bf(rw_w1[ir]),
                               bf(rw_w2[ir]), rw_a0[ir], bf(rw_a1[ir]), bf(rw_a2[ir]), bf(rw_g1[ir]),
                               bf(rw_g2[ir]), rw_kk[ir], rw_ka[ir], rw_rk[ir], rw_lnx_g[ir], rw_lnx_b[ir])
            xs = _mix_ffn(xs, y, bf(rw_wo[ir]), *ffn)
            ir += 1
        else:
            xs = _pool_ffn(xs, T, mix_norm[layer], bf(pool_w[ip]), pool_scale[ip], *ffn)
            ip += 1
    return xs.reshape(B, T, C)
```

```python
import functools
import math

import jax
import jax.numpy as jnp
from jax import lax
from jax.experimental import pallas as pl
from jax.experimental.pallas import tpu as pltpu

F32 = jnp.float32
BF16 = jnp.bfloat16

LANES = 128
SUBLANES = 8
VMEM_LIMIT_BYTES = 48 * 1024 * 1024
ATTN_VMEM_LIMIT_BYTES = 54 * 1024 * 1024

CHUNK = 64
HEAD = 64
N_MIXERS = 3
NORM_EPS = 1e-6
QK_EPS = 1e-6
SUBLN_EPS = 1e-5
ROPE_THETA = 10000.0
RW_LNX_EPS = 64e-5
POOL_WINDOWS = (2, 4, 8, 16)
POOL_HALO = 16
NEG = -1e30


def _rms(z, gain, eps):
    return z * lax.rsqrt(jnp.mean(z * z, axis=-1, keepdims=True) + eps) * gain


def _bdot(a, b):
    return jnp.dot(a, b, preferred_element_type=F32)


def _dot_nt(a, b):
    return lax.dot_general(a, b, (((1,), (1,)), ((), ())), preferred_element_type=F32)


def _dot_tn(a, b):
    return lax.dot_general(a, b, (((0,), (0,)), ((), ())), preferred_element_type=F32)


def _half_sums(z, first):
    lo = jnp.sum(jnp.where(first, z, 0.0), axis=-1, keepdims=True)
    hi = jnp.sum(jnp.where(first, 0.0, z), axis=-1, keepdims=True)
    return jnp.where(first, lo, hi)


def _params(*sem, vmem_limit_bytes=VMEM_LIMIT_BYTES):
    return pltpu.CompilerParams(dimension_semantics=sem, vmem_limit_bytes=vmem_limit_bytes)


def _resident(shape):
    return pl.BlockSpec(shape, lambda *_: (0,) * len(shape), pipeline_mode=pl.Buffered(1))


def _ffn_residual(x1, g_ref, win_ref, wout_ref, o_ref, hidden, tf):
    hn = _rms(x1, g_ref[...], NORM_EPS).astype(BF16)
    acc = x1
    for f in range(hidden // tf):
        gate = _bdot(hn, win_ref[:, f * tf:(f + 1) * tf])
        up = _bdot(hn, win_ref[:, hidden + f * tf:hidden + (f + 1) * tf])
        act = (gate * jax.nn.sigmoid(gate) * up).astype(BF16)
        acc = acc + _bdot(act, wout_ref[f * tf:(f + 1) * tf, :])
    o_ref[...] = acc


def _mix_ffn_kernel(x_ref, y_ref, wm_ref, g_ref, win_ref, wout_ref, o_ref, *, hidden, tf):
    x1 = x_ref[...] + _bdot(y_ref[...], wm_ref[...])
    _ffn_residual(x1, g_ref, win_ref, wout_ref, o_ref, hidden, tf)


def _mix_ffn(x, y, w_mix, g, w_in, w_out, *, tm=1024, tf=256):
    M, C = x.shape
    hidden = w_out.shape[0]
    row = pl.BlockSpec((tm, C), lambda i: (i, 0))
    return pl.pallas_call(
        functools.partial(_mix_ffn_kernel, hidden=hidden, tf=tf),
        grid=(M // tm,),
        in_specs=[row, row, _resident((C, C)), _resident((1, C)), _resident((C, 2 * hidden)),
                  _resident((hidden, C))],
        out_specs=row,
        out_shape=jax.ShapeDtypeStruct((M, C), F32),
        compiler_params=_params("parallel"),
        name="mix_ffn",
    )(x, y, w_mix, g.reshape(1, C), w_in, w_out)


def _qkv_kernel(x_ref, g_ref, wqt_ref, wk_ref, wvt_ref, qgt_ref, kg_ref, cos_ref, sin_ref, cost_ref, sint_ref,
                qt_ref, k_ref, vt_ref):
    tm, C = x_ref.shape
    hn = _rms(x_ref[...], g_ref[...], NORM_EPS).astype(BF16)

    k = _bdot(hn, wk_ref[...])
    vt_ref[0, 0] = _dot_nt(wvt_ref[...], hn).astype(BF16)

    q3 = _dot_nt(wqt_ref[...], hn).reshape(C // HEAD, HEAD, tm)
    ms = jnp.mean(q3 * q3, axis=1, keepdims=True)
    qn = q3 * lax.rsqrt(ms + QK_EPS) * qgt_ref[...][None]
    partner = jnp.concatenate([qn[:, HEAD // 2:], qn[:, :HEAD // 2]], axis=1)
    q3 = qn * cost_ref[...][None] + partner * sint_ref[...][None]
    qt_ref[0, 0] = q3.reshape(C, tm).astype(BF16)

    cos = cos_ref[...]
    sin = sin_ref[...]
    lane = lax.broadcasted_iota(jnp.int32, cos.shape, 1)
    first = lane < HEAD
    low_half = (lane & (HEAD - 1)) < HEAD // 2
    for s in range(C // LANES):
        sl = slice(s * LANES, (s + 1) * LANES)
        z = k[:, sl]
        zn = z * lax.rsqrt(_half_sums(z * z, first) * (1.0 / HEAD) + QK_EPS) * kg_ref[...]
        partner = jnp.where(low_half, pltpu.roll(zn, LANES - HEAD // 2, 1), pltpu.roll(zn, HEAD // 2, 1))
        k_ref[:, sl] = (zn * cos + partner * sin).astype(BF16)


def _attn_kernel(qt_ref, k_ref, vt_ref, lam_ref, sg_ref, o_ref, s_refs, m_refs, l_refs, acc_refs,
                 *, tq, lambda_init):
    i = pl.program_id(2)
    feat = lax.broadcasted_iota(jnp.int32, (LANES, tq), 0)

    def stacked_q(h):
        qt = qt_ref[0, 0, h * LANES:(h + 1) * LANES, :]
        zero = jnp.zeros_like(qt)
        return jnp.concatenate([jnp.where(feat < HEAD, qt, zero), jnp.where(feat < HEAD, zero, qt)], axis=1)

    heads = s_refs.shape[0]
    qst = [stacked_q(h) for h in range(heads)]

    def scores(h, j, diagonal=False):
        start = pl.multiple_of(j * tq, tq)
        s = _bdot(k_ref[0, pl.ds(start, tq), h * LANES:(h + 1) * LANES], qst[h])
        if diagonal:
            key = lax.broadcasted_iota(jnp.int32, s.shape, 0)
            qry = lax.broadcasted_iota(jnp.int32, s.shape, 1)
            qry = jnp.where(qry >= tq, qry - tq, qry)
            s = jnp.where((key // CHUNK) <= (qry // CHUNK), s, NEG)
        s_refs[h] = s

    def absorb(h, j):
        m = m_refs[h]
        m_new = jnp.maximum(m, jnp.max(s_refs[h], axis=0, keepdims=True))
        alpha = jnp.exp2(m - m_new)
        p = jnp.exp2(s_refs[h] - m_new)
        m_refs[h] = m_new
        l_refs[h] = alpha * l_refs[h] + jnp.sum(p, axis=0, keepdims=True)
        vt = vt_ref[0, j, h * LANES:(h + 1) * LANES, :]
        acc_refs[h] = alpha * acc_refs[h] + _bdot(vt, p.astype(BF16))

    def finish(h):
        o = acc_refs[h] / l_refs[h]
        lv = lam_ref[...]
        lam = (jnp.exp(jnp.sum(lv[0:1] * lv[1:2], axis=-1, keepdims=True))
               - jnp.exp(jnp.sum(lv[2:3] * lv[3:4], axis=-1, keepdims=True)) + lambda_init)
        d = (o[:, :tq] - lam * o[:, tq:]).T
        o_ref[0, :, h * LANES:(h + 1) * LANES] = (_rms(d, sg_ref[...], SUBLN_EPS)
                                                  * (1.0 - lambda_init)).astype(BF16)

    m_refs[...] = jnp.full(m_refs.shape, NEG, F32)
    l_refs[...] = jnp.zeros(l_refs.shape, F32)
    acc_refs[...] = jnp.zeros(acc_refs.shape, F32)
    for h in range(heads):
        scores(h, i, diagonal=True)
    for h in range(heads - 1):
        absorb(h, i)

    def step(n, _):
        scores(0, n)
        absorb(heads - 1, jnp.where(n == 0, i, n - 1))
        for h in range(1, heads):
            scores(h, n)
            absorb(h - 1, n)
        return 0

    lax.fori_loop(0, i, step, 0)
    absorb(heads - 1, jnp.where(i == 0, i, i - 1))
    for h in range(heads):
        finish(h)


def _rope_tables(T):
    half = HEAD // 2
    inv = 1.0 / (ROPE_THETA ** (jnp.arange(0, HEAD, 2, dtype=F32) / HEAD))
    ang = jnp.arange(T, dtype=F32)[:, None] * inv[None, :]
    cos, sin = jnp.cos(ang), jnp.sin(ang)
    cos_h = jnp.concatenate([cos, cos], axis=1)
    sin_h = jnp.concatenate([-sin, sin], axis=1)
    reps = LANES // HEAD
    return jnp.tile(cos_h, (1, reps)), jnp.tile(sin_h, (1, reps)), cos_h.T, sin_h.T


def _diff_attention(x, B, T, g, w_qkv, q_gain, k_gain, lam_vec, subln_g, lambda_init, *, tq=512, heads=4):
    M, C = x.shape
    H = C // LANES
    nt = T // tq
    cos_t, sin_t, cos_tt, sin_tt = _rope_tables(T)
    qgt = jnp.broadcast_to((q_gain * (HEAD ** -0.5 * math.log2(math.e)))[:, None], (HEAD, tq))
    kg = jnp.tile(k_gain, LANES // HEAD).reshape(1, LANES)
    wqt, wk, wvt = w_qkv[:, :C].T, w_qkv[:, C:2 * C], w_qkv[:, 2 * C:].T
    row = pl.BlockSpec((tq, C), lambda i: (i, 0))
    table = pl.BlockSpec((tq, LANES), lambda i: (i % nt, 0))
    table_t = pl.BlockSpec((HEAD, tq), lambda i: (0, i % nt))
    transposed = pl.BlockSpec((1, 1, C, tq), lambda i: (i // nt, i % nt, 0, 0))
    qt, k, vt = pl.pallas_call(
        _qkv_kernel,
        grid=(M // tq,),
        in_specs=[row, _resident((1, C)), _resident((C, C)), _resident((C, C)), _resident((C, C)),
                  _resident((HEAD, tq)), _resident((1, LANES)), table, table, table_t, table_t],
        out_specs=[transposed, row, transposed],
        out_shape=[jax.ShapeDtypeStruct((B, nt, C, tq), BF16), jax.ShapeDtypeStruct((M, C), BF16),
                   jax.ShapeDtypeStruct((B, nt, C, tq), BF16)],
        compiler_params=_params("parallel"),
        name="attn_qkv",
    )(x, g.reshape(1, C), wqt, wk, wvt, qgt, kg, cos_t, sin_t, cos_tt, sin_tt)
    o = pl.pallas_call(
        functools.partial(_attn_kernel, tq=tq, lambda_init=lambda_init),
        grid=(B, H // heads, nt),
        in_specs=[pl.BlockSpec((1, 1, heads * LANES, tq), lambda b, h, i: (b, i, h, 0)),
                  pl.BlockSpec((1, T, heads * LANES), lambda b, h, i: (b, 0, h)),
                  pl.BlockSpec((1, nt, heads * LANES, tq), lambda b, h, i: (b, 0, h, 0)),
                  _resident((4, HEAD)), _resident((1, LANES))],
        out_specs=pl.BlockSpec((1, tq, heads * LANES), lambda b, h, i: (b, i, h)),
        out_shape=jax.ShapeDtypeStruct((B, T, C), BF16),
        scratch_shapes=[pltpu.VMEM((heads, tq, 2 * tq), F32), pltpu.VMEM((heads, 1, 2 * tq), F32),
                        pltpu.VMEM((heads, 1, 2 * tq), F32), pltpu.VMEM((heads, LANES, 2 * tq), F32)],
        compiler_params=_params("parallel", "parallel", "arbitrary", vmem_limit_bytes=ATTN_VMEM_LIMIT_BYTES),
        name="attn_core",
    )(qt, k.reshape(B, T, C), vt, lam_vec, subln_g.reshape(1, LANES))
    return o.reshape(M, C)


def _rwkv_prep_kernel(x_ref, xp_ref, g_ref, mix_ref, wrkv_ref, w1_ref, w2_ref, a1_ref, a2_ref, g1_ref, g2_ref,
                      w0_ref, a0_ref, kk_ref, ka_ref,
                      r_out, lw_out, k_out, v_out, an_out, bn_out, g_out, *, tiles_per_seq):
    i = pl.program_id(0)
    tm, C = x_ref.shape
    g = g_ref[...]
    h = _rms(x_ref[...], g, NORM_EPS)
    h_last = _rms(xp_ref[SUBLANES - 1:SUBLANES, :], g, NORM_EPS)
    h_last = jnp.where(i % tiles_per_seq == 0, 0.0, h_last)
    row = lax.broadcasted_iota(jnp.int32, h.shape, 0)
    dx = jnp.where(row == 0, h_last, pltpu.roll(h, 1, 0)) - h
    mix = mix_ref[...]

    def mixed(n):
        return (h + dx * mix[n:n + 1]).astype(BF16)

    a_lora = _bdot(_bdot(mixed(4), a1_ref[...]).astype(BF16), a2_ref[...])
    k = _bdot(mixed(2), wrkv_ref[:, C:2 * C])
    w_lora = _bdot(jnp.tanh(_bdot(mixed(1), w1_ref[...])).astype(BF16), w2_ref[...])
    g_out[...] = _bdot(jax.nn.sigmoid(_bdot(mixed(5), g1_ref[...])).astype(BF16), g2_ref[...])
    r_out[...] = _bdot(mixed(0), wrkv_ref[:, 0:C])
    v_out[...] = _bdot(mixed(3), wrkv_ref[:, 2 * C:])

    lw_out[...] = -math.exp(-0.5) * jax.nn.sigmoid(w0_ref[...] + w_lora)
    a = jax.nn.sigmoid(a0_ref[...] + a_lora)
    kk = k * kk_ref[...]
    k_out[...] = k * (1.0 + (a - 1.0) * ka_ref[...])

    lane = lax.broadcasted_iota(jnp.int32, (tm, LANES), 1)
    first = lane < HEAD
    for s in range(C // LANES):
        sl = slice(s * LANES, (s + 1) * LANES)
        kks = kk[:, sl]
        kkn = kks * lax.rsqrt(jnp.maximum(_half_sums(kks * kks, first), 1e-24))
        an_out[:, sl] = -kkn
        bn_out[:, sl] = kkn * a[:, sl]


def _rwkv_scan_kernel(r_ref, lw_ref, k_ref, v_ref, a_ref, b_ref, g_ref, rk_ref, lg_ref, lb_ref, o_ref,
                      h_ref, qe_s, yl_s, mc_s, gc_s, bonus_s, gate_s, *, n_chunks, blocks_per_seq):
    t = pl.program_id(1)
    last = pl.num_programs(1) - 1
    carried = (qe_s, yl_s, mc_s, gc_s, bonus_s, gate_s)
    inputs = (r_ref, lw_ref, k_ref, v_ref, a_ref, b_ref, g_ref, rk_ref)

    def chain():
        return _rwkv_chain(lg_ref, lb_ref, o_ref, h_ref, *carried, n_chunks=n_chunks,
                           restart=(t - 1) % blocks_per_seq == 0)

    @pl.when(t == 0)
    def _():
        h_ref[...] = jnp.zeros_like(h_ref)
        _rwkv_chunks(*inputs, *carried, n_chunks=n_chunks, chain=iter(()))

    @pl.when((t > 0) & (t < last))
    def _():
        _rwkv_chunks(*inputs, *carried, n_chunks=n_chunks, chain=chain())

    @pl.when(t == last)
    def _():
        for _ in chain():
            pass


def _rwkv_chunks(r_ref, lw_ref, k_ref, v_ref, a_ref, b_ref, g_ref, rk_ref, qe_s, yl_s, mc_s, gc_s, bonus_s, gate_s,
                 *, n_chunks, chain):
    L = CHUNK

    row = lax.broadcasted_iota(jnp.int32, (2 * L, LANES), 0)
    col = lax.broadcasted_iota(jnp.int32, (2 * L, LANES), 1)
    block_diag = (row >= L) == (col >= L)
    rt = row & (L - 1)
    ct = col & (L - 1)
    strict = ct < rt
    incl = ct <= rt
    eye = (row == col).astype(F32)
    lane = lax.broadcasted_iota(jnp.int32, (L, LANES), 1)
    first = lane < HEAD
    tr = lax.broadcasted_iota(jnp.int32, (L, L), 0)
    tc = lax.broadcasted_iota(jnp.int32, (L, L), 1)
    tri = (tc <= tr).astype(BF16)

    def stack(z):
        return jnp.concatenate([jnp.where(first, z, 0.0), jnp.where(first, 0.0, z)], axis=0)

    def split3(z):
        hi = z.astype(BF16)
        r1 = z - hi.astype(F32)
        mid = r1.astype(BF16)
        return hi, mid, (r1 - mid.astype(F32)).astype(BF16)

    rk = rk_ref[...]

    chunks = range(n_chunks)
    sls = [pl.ds(c * L, L) for c in chunks]
    R = [r_ref[0, sl, :] for sl in sls]
    LW = [lw_ref[0, sl, :] for sl in sls]
    K = [k_ref[0, sl, :] for sl in sls]
    V = [v_ref[0, sl, :] for sl in sls]
    A = [a_ref[0, sl, :] for sl in sls]
    Bv = [b_ref[0, sl, :] for sl in sls]
    def stage(fn, ticks=1):
        out = [fn(c) for c in chunks]
        for _ in range(ticks):
            next(chain, None)
        return out

    cs = stage(lambda c: _bdot(tri, jnp.concatenate(split3(LW[c]), axis=1)), ticks=2)
    cum = [z[:, :LANES] + z[:, LANES:2 * LANES] + z[:, 2 * LANES:] for z in cs]
    cum_last = [z[L - 1:L, :] for z in cum]
    inv = [jnp.exp(-z) for z in cum]
    to_end = [jnp.exp(cum_last[c] - cum[c]) for c in chunks]
    Rt = [R[c] * jnp.exp(cum[c]) for c in chunks]
    At = [A[c] * jnp.exp(cum[c] - LW[c]) for c in chunks]
    Kt = [K[c] * inv[c] for c in chunks]
    Bt = [Bv[c] * inv[c] for c in chunks]
    AtS = [stack(z) for z in At]
    RtS = [stack(z) for z in Rt]
    BhT = [stack(Bv[c] * to_end[c]).T.astype(BF16) for c in chunks]
    KhT = [stack(K[c] * to_end[c]).T.astype(BF16) for c in chunks]
    kv = stage(lambda c: _bdot(KhT[c], stack(V[c]).astype(BF16)), ticks=2)

    p0 = stage(lambda c: _dot_nt(jnp.concatenate([AtS[c][:L], RtS[c][:L]], axis=0).astype(BF16),
                                 jnp.concatenate([Bt[c], Kt[c]], axis=0).astype(BF16)), ticks=2)
    p1 = stage(lambda c: _dot_nt(jnp.concatenate([AtS[c][L:], RtS[c][L:]], axis=0).astype(BF16),
                                 jnp.concatenate([Kt[c], Bt[c]], axis=0).astype(BF16)), ticks=2)
    top = [jnp.concatenate([p0[c][:L], p1[c][:L]], axis=0) for c in chunks]
    bot = [jnp.concatenate([p0[c][L:], p1[c][L:]], axis=0) for c in chunks]
    n_ab = [jnp.where(block_diag & strict, z, 0.0) for z in top]
    m_ak = [jnp.where(block_diag | ~strict, 0.0, z) for z in top]
    m_rb = [jnp.where(block_diag & incl, z, 0.0) for z in bot]
    m_rk = [jnp.where(block_diag | ~incl, 0.0, z) for z in bot]

    tinv = [eye + z for z in n_ab]
    xb = [z.astype(BF16) for z in n_ab]
    xb = [z.astype(BF16) for z in stage(lambda c: _bdot(xb[c], xb[c]))]
    for _ in range(4):
        xt = stage(lambda c: _bdot(xb[c], jnp.concatenate([xb[c], tinv[c].astype(BF16)], axis=1)))
        xb = [z[:, :LANES].astype(BF16) for z in xt]
        tinv = [tinv[c] + xt[c][:, LANES:] for c in chunks]
    last = stage(lambda c: _bdot(xb[c], tinv[c].astype(BF16)))
    tinv = [tinv[c] + last[c] for c in chunks]

    Vb = [z.astype(BF16) for z in V]
    zy = stage(lambda c: _bdot(jnp.concatenate([m_ak[c], m_rk[c]], axis=0).astype(BF16),
                               jnp.concatenate([Vb[c], Vb[c]], axis=0)))
    z_ak = [jnp.where(block_diag, z[:2 * L], 0.0) for z in zy]
    y_rk = [jnp.where(block_diag, z[2 * L:], 0.0) for z in zy]
    wub = [z.astype(BF16) for z in stage(
        lambda c: _bdot(tinv[c].astype(BF16), jnp.concatenate([AtS[c], z_ak[c]], axis=1).astype(BF16)))]
    qy = stage(lambda c: _bdot(m_rb[c].astype(BF16), wub[c]))
    q_eff = [(RtS[c] + qy[c][:, :LANES]).astype(BF16) for c in chunks]
    y_loc = [y_rk[c] + qy[c][:, LANES:] for c in chunks]
    mg = stage(lambda c: _bdot(BhT[c], wub[c]))
    m_c = [(eye * jnp.exp(cum_last[c]) + mg[c][:, :LANES]).astype(BF16) for c in chunks]
    g_c = [mg[c][:, LANES:] + kv[c] for c in chunks]

    for _ in chain:
        pass
    for c in chunks:
        qe_s[c] = q_eff[c]
        yl_s[c] = y_loc[c]
        mc_s[c] = m_c[c]
        gc_s[c] = g_c[c]
        bonus_s[c] = _half_sums(R[c] * K[c] * rk, first) * V[c]
        gate_s[c] = g_ref[0, sls[c], :]


def _rwkv_chain(lg_ref, lb_ref, o_ref, h_ref, qe_s, yl_s, mc_s, gc_s, bonus_s, gate_s, *, n_chunks, restart):
    L = CHUNK
    lane = lax.broadcasted_iota(jnp.int32, (L, LANES), 1)
    first = lane < HEAD
    lg = lg_ref[...]
    lb = lb_ref[...]
    h = jnp.where(restart, 0.0, h_ref[...])
    ys = []
    for c in range(n_chunks):
        both = _bdot(jnp.concatenate([qe_s[c], mc_s[c]], axis=0), h.astype(BF16))
        ys.append(both[:2 * L] + yl_s[c])
        h = both[2 * L:] + gc_s[c]
        yield
    h_ref[...] = h

    for c in range(n_chunks):
        y = ys[c][:L] + ys[c][L:]
        mu = _half_sums(y, first) * (1.0 / HEAD)
        d = y - mu
        var = _half_sums(d * d, first) * (1.0 / HEAD)
        yn = d * lax.rsqrt(var + RW_LNX_EPS) * lg + lb
        o_ref[0, pl.ds(c * L, L), :] = ((yn + bonus_s[c]) * gate_s[c]).astype(BF16)


def _rwkv_time_mix(x, B, T, g, mix, w_rkv, w0, w1, w2, a0, a1, a2, g1, g2, k_k, k_a, r_k, lnx_g, lnx_b,
                   *, tm=512, tt=1024):
    M, C = x.shape
    row = pl.BlockSpec((tm, C), lambda i: (i, 0))
    prev = pl.BlockSpec((SUBLANES, C), lambda i: (jnp.maximum(i * (tm // SUBLANES) - 1, 0), 0))
    vec = lambda a: a.reshape(1, C)
    outs = pl.pallas_call(
        functools.partial(_rwkv_prep_kernel, tiles_per_seq=T // tm),
        grid=(M // tm,),
        in_specs=[row, prev, _resident((1, C)), _resident(mix.shape), _resident(w_rkv.shape),
                  _resident(w1.shape), _resident(w2.shape), _resident(a1.shape), _resident(a2.shape),
                  _resident(g1.shape), _resident(g2.shape)] + [_resident((1, C))] * 4,
        out_specs=[row] * 7,
        out_shape=[jax.ShapeDtypeStruct((M, C), F32)] * 7,
        compiler_params=_params("parallel"),
        name="rwkv_prep",
    )(x, x, vec(g), mix, w_rkv, w1, w2, a1, a2, g1, g2, vec(w0), vec(a0), vec(k_k), vec(k_a))
    seq = [z.reshape(B, T, C) for z in outs]
    nt = T // tt
    n_chunks = tt // CHUNK
    def block(t):
        return t // nt, t % nt

    slab_in = pl.BlockSpec((1, tt, LANES), lambda s, t: (*block(jnp.minimum(t, B * nt - 1)), s))
    slab_out = pl.BlockSpec((1, tt, LANES), lambda s, t: (*block(jnp.maximum(t - 1, 0)), s))
    pvec = pl.BlockSpec((1, LANES), lambda s, t: (0, s))
    stacked = (n_chunks, 2 * CHUNK, LANES)
    y = pl.pallas_call(
        functools.partial(_rwkv_scan_kernel, n_chunks=n_chunks, blocks_per_seq=nt),
        grid=(C // LANES, B * nt + 1),
        in_specs=[slab_in] * 7 + [pvec] * 3,
        out_specs=slab_out,
        out_shape=jax.ShapeDtypeStruct((B, T, C), BF16),
        scratch_shapes=[pltpu.VMEM((LANES, LANES), F32),
                        pltpu.VMEM(stacked, BF16), pltpu.VMEM(stacked, F32),
                        pltpu.VMEM(stacked, BF16), pltpu.VMEM(stacked, F32),
                        pltpu.VMEM((n_chunks, CHUNK, LANES), F32), pltpu.VMEM((n_chunks, CHUNK, LANES), F32)],
        compiler_params=_params("parallel", "arbitrary"),
        name="rwkv_scan",
    )(*seq, vec(r_k), vec(lnx_g), vec(lnx_b))
    return y.reshape(M, C)


def _pool_ffn_kernel(x_ref, xh_ref, gm_ref, pw_ref, sc_ref, g_ref, win_ref, wout_ref, o_ref,
                     *, hidden, tf, tiles_per_seq):
    ti = pl.program_id(0) % tiles_per_seq
    tm, C = x_ref.shape
    group = C // len(POOL_WINDOWS)
    gm = gm_ref[...]
    x = x_ref[...]
    h = _rms(x, gm, NORM_EPS)
    halo = jnp.where(ti == 0, 0.0, _rms(xh_ref[...], gm, NORM_EPS))
    he = jnp.concatenate([halo, h], axis=0)
    t = ti * tm + lax.broadcasted_iota(jnp.int32, (tm, 1), 0)
    ys = []
    for gi, win in enumerate(POOL_WINDOWS):
        sl = slice(gi * group, (gi + 1) * group)
        s = he[:, sl]
        shift = 1
        while shift < win:
            s = s + pltpu.roll(s, shift, 0)
            shift *= 2
        cnt = jnp.minimum(t + 1, win).astype(F32)
        delta = (s[POOL_HALO:] / cnt - h[:, sl]).astype(BF16)
        ys.append(_bdot(delta, pw_ref[gi]))
    x1 = x + jnp.concatenate(ys, axis=1) * sc_ref[...]
    _ffn_residual(x1, g_ref, win_ref, wout_ref, o_ref, hidden, tf)


def _pool_ffn(x, T, g_mix, pool_w, scale, g, w_in, w_out, *, tm=512, tf=256):
    M, C = x.shape
    hidden = w_out.shape[0]
    blocks = tm // POOL_HALO
    row = pl.BlockSpec((tm, C), lambda i: (i, 0))
    return pl.pallas_call(
        functools.partial(_pool_ffn_kernel, hidden=hidden, tf=tf, tiles_per_seq=T // tm),
        grid=(M // tm,),
        in_specs=[row, pl.BlockSpec((POOL_HALO, C), lambda i: (jnp.maximum(i * blocks - 1, 0), 0)),
                  _resident((1, C)), _resident(pool_w.shape), _resident((1, C)), _resident((1, C)),
                  _resident((C, 2 * hidden)), _resident((hidden, C))],
        out_specs=row,
        out_shape=jax.ShapeDtypeStruct((M, C), F32),
        compiler_params=_params("parallel"),
        name="pool_ffn",
    )(x, x, g_mix.reshape(1, C), pool_w, scale.reshape(1, C), g.reshape(1, C), w_in, w_out)


def kernel(x, mix_norm, ffn_norm, ffn_w_in, ffn_w_out, da_wqkv, da_wo, da_q_gain, da_k_gain, da_lambda, da_subln, rw_mix, rw_wrkv, rw_wo, rw_w0, rw_w1, rw_w2, rw_a0, rw_a1, rw_a2, rw_g1, rw_g2, rw_kk, rw_ka, rw_rk, rw_lnx_g, rw_lnx_b, pool_w, pool_scale):
    B, T, C = x.shape
    depth = mix_norm.shape[0]
    bf = lambda w: w.astype(BF16)
    xs = x.reshape(B * T, C)
    ia = ir = ip = 0
    for layer in range(depth):
        kind = layer % N_MIXERS
        ffn = (ffn_norm[layer], bf(ffn_w_in[layer]), bf(ffn_w_out[layer]))
        if kind == 0:
            lambda_init = 0.8 - 0.6 * math.exp(-0.3 * layer)
            y = _diff_attention(xs, B, T, mix_norm[layer], bf(da_wqkv[ia]), da_q_gain[ia], da_k_gain[ia],
                                da_lambda[ia], da_subln[ia], lambda_init)
            xs = _mix_ffn(xs, y, bf(da_wo[ia]), *ffn)
            ia += 1
        elif kind == 1:
            y = _rwkv_time_mix(xs, B, T, mix_norm[layer], rw_mix[ir], bf(rw_wrkv[ir]), rw_w0[ir], bf(rw_w1[ir]),
                               bf(rw_w2[ir]), rw_a0[ir], bf(rw_a1[ir]), bf(rw_a2[ir]), bf(rw_g1[ir]),
                               bf(rw_g2[ir]), rw_kk[ir], rw_ka[ir], rw_rk[ir], rw_lnx_g[ir], rw_lnx_b[ir])
            xs = _mix_ffn(xs, y, bf(rw_wo[ir]), *ffn)
            ir += 1
        else:
            xs = _pool_ffn(xs, T, mix_norm[layer], bf(pool_w[ip]), pool_scale[ip], *ffn)
            ip += 1
    return xs.reshape(B, T, C)
```

```python
import functools
import math

import jax
import jax.numpy as jnp
from jax import lax
from jax.experimental import pallas as pl
from jax.experimental.pallas import tpu as pltpu

F32 = jnp.float32
BF16 = jnp.bfloat16

LANES = 128
SUBLANES = 8
VMEM_LIMIT_BYTES = 48 * 1024 * 1024
ATTN_VMEM_LIMIT_BYTES = 54 * 1024 * 1024

CHUNK = 64
HEAD = 64
N_MIXERS = 3
NORM_EPS = 1e-6
QK_EPS = 1e-6
SUBLN_EPS = 1e-5
ROPE_THETA = 10000.0
RW_LNX_EPS = 64e-5
POOL_WINDOWS = (2, 4, 8, 16)
POOL_HALO = 16
NEG = -1e30


def _rms(z, gain, eps):
    return z * lax.rsqrt(jnp.mean(z * z, axis=-1, keepdims=True) + eps) * gain


def _bdot(a, b):
    return jnp.dot(a, b, preferred_element_type=F32)


def _dot_nt(a, b):
    return lax.dot_general(a, b, (((1,), (1,)), ((), ())), preferred_element_type=F32)


def _dot_tn(a, b):
    return lax.dot_general(a, b, (((0,), (0,)), ((), ())), preferred_element_type=F32)


def _half_sums(z, first):
    lo = jnp.sum(jnp.where(first, z, 0.0), axis=-1, keepdims=True)
    hi = jnp.sum(jnp.where(first, 0.0, z), axis=-1, keepdims=True)
    return jnp.where(first, lo, hi)


def _params(*sem, vmem_limit_bytes=VMEM_LIMIT_BYTES):
    return pltpu.CompilerParams(dimension_semantics=sem, vmem_limit_bytes=vmem_limit_bytes)


def _resident(shape):
    return pl.BlockSpec(shape, lambda *_: (0,) * len(shape), pipeline_mode=pl.Buffered(1))


def _ffn_residual(x1, g_ref, win_ref, wout_ref, o_ref, hidden, tf):
    hn = _rms(x1, g_ref[...], NORM_EPS).astype(BF16)
    acc = x1
    for f in range(hidden // tf):
        gate = _bdot(hn, win_ref[:, f * tf:(f + 1) * tf])
        up = _bdot(hn, win_ref[:, hidden + f * tf:hidden + (f + 1) * tf])
        act = (gate * jax.nn.sigmoid(gate) * up).astype(BF16)
        acc = acc + _bdot(act, wout_ref[f * tf:(f + 1) * tf, :])
    o_ref[...] = acc


def _mix_ffn_kernel(x_ref, y_ref, wm_ref, g_ref, win_ref, wout_ref, o_ref, *, hidden, tf):
    x1 = x_ref[...] + _bdot(y_ref[...], wm_ref[...])
    _ffn_residual(x1, g_ref, win_ref, wout_ref, o_ref, hidden, tf)


def _mix_ffn(x, y, w_mix, g, w_in, w_out, *, tm=1024, tf=256):
    M, C = x.shape
    hidden = w_out.shape[0]
    row = pl.BlockSpec((tm, C), lambda i: (i, 0))
    return pl.pallas_call(
        functools.partial(_mix_ffn_kernel, hidden=hidden, tf=tf),
        grid=(M // tm,),
        in_specs=[row, row, _resident((C, C)), _resident((1, C)), _resident((C, 2 * hidden)),
                  _resident((hidden, C))],
        out_specs=row,
        out_shape=jax.ShapeDtypeStruct((M, C), F32),
        compiler_params=_params("parallel"),
        name="mix_ffn",
    )(x, y, w_mix, g.reshape(1, C), w_in, w_out)


def _qkv_kernel(x_ref, g_ref, wqt_ref, wk_ref, wvt_ref, qgt_ref, kg_ref, cos_ref, sin_ref, cost_ref, sint_ref,
                qt_ref, k_ref, vt_ref):
    tm, C = x_ref.shape
    hn = _rms(x_ref[...], g_ref[...], NORM_EPS).astype(BF16)

    k = _bdot(hn, wk_ref[...])
    vt_ref[0, 0] = _dot_nt(wvt_ref[...], hn).astype(BF16)

    q3 = _dot_nt(wqt_ref[...], hn).reshape(C // HEAD, HEAD, tm)
    ms = jnp.mean(q3 * q3, axis=1, keepdims=True)
    qn = q3 * lax.rsqrt(ms + QK_EPS) * qgt_ref[...][None]
    partner = jnp.concatenate([qn[:, HEAD // 2:], qn[:, :HEAD // 2]], axis=1)
    q3 = qn * cost_ref[...][None] + partner * sint_ref[...][None]
    qt_ref[0, 0] = q3.reshape(C, tm).astype(BF16)

    cos = cos_ref[...]
    sin = sin_ref[...]
    lane = lax.broadcasted_iota(jnp.int32, cos.shape, 1)
    first = lane < HEAD
    low_half = (lane & (HEAD - 1)) < HEAD // 2
    for s in range(C // LANES):
        sl = slice(s * LANES, (s + 1) * LANES)
        z = k[:, sl]
        zn = z * lax.rsqrt(_half_sums(z * z, first) * (1.0 / HEAD) + QK_EPS) * kg_ref[...]
        partner = jnp.where(low_half, pltpu.roll(zn, LANES - HEAD // 2, 1), pltpu.roll(zn, HEAD // 2, 1))
        k_ref[:, sl] = (zn * cos + partner * sin).astype(BF16)


def _attn_kernel(qt_ref, k_ref, vt_ref, lam_ref, sg_ref, o_ref, s_refs, m_refs, l_refs, acc_refs,
                 *, tq, lambda_init):
    i = pl.program_id(2)
    feat = lax.broadcasted_iota(jnp.int32, (LANES, tq), 0)

    def stacked_q(h):
        qt = qt_ref[0, 0, h * LANES:(h + 1) * LANES, :]
        zero = jnp.zeros_like(qt)
        return jnp.concatenate([jnp.where(feat < HEAD, qt, zero), jnp.where(feat < HEAD, zero, qt)], axis=1)

    heads = s_refs.shape[0]
    qst = [stacked_q(h) for h in range(heads)]

    def scores(h, j, diagonal=False):
        start = pl.multiple_of(j * tq, tq)
        s = _bdot(k_ref[0, pl.ds(start, tq), h * LANES:(h + 1) * LANES], qst[h])
        if diagonal:
            key = lax.broadcasted_iota(jnp.int32, s.shape, 0)
            qry = lax.broadcasted_iota(jnp.int32, s.shape, 1)
            qry = jnp.where(qry >= tq, qry - tq, qry)
            s = jnp.where((key // CHUNK) <= (qry // CHUNK), s, NEG)
        s_refs[h] = s

    def absorb(h, j):
        m = m_refs[h]
        m_new = jnp.maximum(m, jnp.max(s_refs[h], axis=0, keepdims=True))
        alpha = jnp.exp2(m - m_new)
        p = jnp.exp2(s_refs[h] - m_new)
        m_refs[h] = m_new
        l_refs[h] = alpha * l_refs[h] + jnp.sum(p, axis=0, keepdims=True)
        vt = vt_ref[0, j, h * LANES:(h + 1) * LANES, :]
        acc_refs[h] = alpha * acc_refs[h] + _bdot(vt, p.astype(BF16))

    def finish(h):
        o = acc_refs[h] / l_refs[h]
        lv = lam_ref[...]
        lam = (jnp.exp(jnp.sum(lv[0:1] * lv[1:2], axis=-1, keepdims=True))
               - jnp.exp(jnp.sum(lv[2:3] * lv[3:4], axis=-1, keepdims=True)) + lambda_init)
        d = (o[:, :tq] - lam * o[:, tq:]).T
        o_ref[0, :, h * LANES:(h + 1) * LANES] = (_rms(d, sg_ref[...], SUBLN_EPS)
                                                  * (1.0 - lambda_init)).astype(BF16)

    m_refs[...] = jnp.full(m_refs.shape, NEG, F32)
    l_refs[...] = jnp.zeros(l_refs.shape, F32)
    acc_refs[...] = jnp.zeros(acc_refs.shape, F32)
    for h in range(heads):
        scores(h, i, diagonal=True)
    for h in range(heads - 1):
        absorb(h, i)

    def step(n, _):
        scores(0, n)
        absorb(heads - 1, jnp.where(n == 0, i, n - 1))
        for h in range(1, heads):
            scores(h, n)
            absorb(h - 1, n)
        return 0

    lax.fori_loop(0, i, step, 0)
    absorb(heads - 1, jnp.where(i == 0, i, i - 1))
    for h in range(heads):
        finish(h)


def _rope_tables(T):
    half = HEAD // 2
    inv = 1.0 / (ROPE_THETA ** (jnp.arange(0, HEAD, 2, dtype=F32) / HEAD))
    ang = jnp.arange(T, dtype=F32)[:, None] * inv[None, :]
    cos, sin = jnp.cos(ang), jnp.sin(ang)
    cos_h = jnp.concatenate([cos, cos], axis=1)
    sin_h = jnp.concatenate([-sin, sin], axis=1)
    reps = LANES // HEAD
    return jnp.tile(cos_h, (1, reps)), jnp.tile(sin_h, (1, reps)), cos_h.T, sin_h.T


def _diff_attention(x, B, T, g, w_qkv, q_gain, k_gain, lam_vec, subln_g, lambda_init, *, tq=512, heads=4):
    M, C = x.shape
    H = C // LANES
    nt = T // tq
    cos_t, sin_t, cos_tt, sin_tt = _rope_tables(T)
    qgt = jnp.broadcast_to((q_gain * (HEAD ** -0.5 * math.log2(math.e)))[:, None], (HEAD, tq))
    kg = jnp.tile(k_gain, LANES // HEAD).reshape(1, LANES)
    wqt, wk, wvt = w_qkv[:, :C].T, w_qkv[:, C:2 * C], w_qkv[:, 2 * C:].T
    row = pl.BlockSpec((tq, C), lambda i: (i, 0))
    table = pl.BlockSpec((tq, LANES), lambda i: (i % nt, 0))
    table_t = pl.BlockSpec((HEAD, tq), lambda i: (0, i % nt))
    transposed = pl.BlockSpec((1, 1, C, tq), lambda i: (i // nt, i % nt, 0, 0))
    qt, k, vt = pl.pallas_call(
        _qkv_kernel,
        grid=(M // tq,),
        in_specs=[row, _resident((1, C)), _resident((C, C)), _resident((C, C)), _resident((C, C)),
                  _resident((HEAD, tq)), _resident((1, LANES)), table, table, table_t, table_t],
        out_specs=[transposed, row, transposed],
        out_shape=[jax.ShapeDtypeStruct((B, nt, C, tq), BF16), jax.ShapeDtypeStruct((M, C), BF16),
                   jax.ShapeDtypeStruct((B, nt, C, tq), BF16)],
        compiler_params=_params("parallel"),
        name="attn_qkv",
    )(x, g.reshape(1, C), wqt, wk, wvt, qgt, kg, cos_t, sin_t, cos_tt, sin_tt)
    o = pl.pallas_call(
        functools.partial(_attn_kernel, tq=tq, lambda_init=lambda_init),
        grid=(B, H // heads, nt),
        in_specs=[pl.BlockSpec((1, 1, heads * LANES, tq), lambda b, h, i: (b, i, h, 0)),
                  pl.BlockSpec((1, T, heads * LANES), lambda b, h, i: (b, 0, h)),
                  pl.BlockSpec((1, nt, heads * LANES, tq), lambda b, h, i: (b, 0, h, 0)),
                  _resident((4, HEAD)), _resident((1, LANES))],
        out_specs=pl.BlockSpec((1, tq, heads * LANES), lambda b, h, i: (b, i, h)),
        out_shape=jax.ShapeDtypeStruct((B, T, C), BF16),
        scratch_shapes=[pltpu.VMEM((heads, tq, 2 * tq), F32), pltpu.VMEM((heads, 1, 2 * tq), F32),
                        pltpu.VMEM((heads, 1, 2 * tq), F32), pltpu.VMEM((heads, LANES, 2 * tq), F32)],
        compiler_params=_params("parallel", "parallel", "arbitrary", vmem_limit_bytes=ATTN_VMEM_LIMIT_BYTES),
        name="attn_core",
    )(qt, k.reshape(B, T, C), vt, lam_vec, subln_g.reshape(1, LANES))
    return o.reshape(M, C)


def _rwkv_prep_kernel(x_ref, xp_ref, g_ref, mix_ref, wrkv_ref, w1_ref, w2_ref, a1_ref, a2_ref, g1_ref, g2_ref,
                      w0_ref, a0_ref, kk_ref, ka_ref,
                      r_out, lw_out, k_out, v_out, an_out, bn_out, g_out, *, tiles_per_seq):
    i = pl.program_id(0)
    tm, C = x_ref.shape
    g = g_ref[...]
    h = _rms(x_ref[...], g, NORM_EPS)
    h_last = _rms(xp_ref[SUBLANES - 1:SUBLANES, :], g, NORM_EPS)
    h_last = jnp.where(i % tiles_per_seq == 0, 0.0, h_last)
    row = lax.broadcasted_iota(jnp.int32, h.shape, 0)
    dx = jnp.where(row == 0, h_last, pltpu.roll(h, 1, 0)) - h
    mix = mix_ref[...]

    def mixed(n):
        return (h + dx * mix[n:n + 1]).astype(BF16)

    a_lora = _bdot(_bdot(mixed(4), a1_ref[...]).astype(BF16), a2_ref[...])
    k = _bdot(mixed(2), wrkv_ref[:, C:2 * C])
    w_lora = _bdot(jnp.tanh(_bdot(mixed(1), w1_ref[...])).astype(BF16), w2_ref[...])
    g_out[...] = _bdot(jax.nn.sigmoid(_bdot(mixed(5), g1_ref[...])).astype(BF16), g2_ref[...])
    r_out[...] = _bdot(mixed(0), wrkv_ref[:, 0:C])
    v_out[...] = _bdot(mixed(3), wrkv_ref[:, 2 * C:])

    lw_out[...] = -math.exp(-0.5) * jax.nn.sigmoid(w0_ref[...] + w_lora)
    a = jax.nn.sigmoid(a0_ref[...] + a_lora)
    kk = k * kk_ref[...]
    k_out[...] = k * (1.0 + (a - 1.0) * ka_ref[...])

    lane = lax.broadcasted_iota(jnp.int32, (tm, LANES), 1)
    first = lane < HEAD
    for s in range(C // LANES):
        sl = slice(s * LANES, (s + 1) * LANES)
        kks = kk[:, sl]
        kkn = kks * lax.rsqrt(jnp.maximum(_half_sums(kks * kks, first), 1e-24))
        an_out[:, sl] = -kkn
        bn_out[:, sl] = kkn * a[:, sl]


def _rwkv_scan_kernel(r_ref, lw_ref, k_ref, v_ref, a_ref, b_ref, g_ref, rk_ref, lg_ref, lb_ref, o_ref,
                      h_ref, qe_s, yl_s, mc_s, gc_s, bonus_s, gate_s, *, n_chunks, blocks_per_seq):
    t = pl.program_id(1)
    last = pl.num_programs(1) - 1
    carried = (qe_s, yl_s, mc_s, gc_s, bonus_s, gate_s)
    inputs = (r_ref, lw_ref, k_ref, v_ref, a_ref, b_ref, g_ref, rk_ref)

    def chain():
        return _rwkv_chain(lg_ref, lb_ref, o_ref, h_ref, *carried, n_chunks=n_chunks,
                           restart=(t - 1) % blocks_per_seq == 0)

    @pl.when(t == 0)
    def _():
        h_ref[...] = jnp.zeros_like(h_ref)
        _rwkv_chunks(*inputs, *carried, n_chunks=n_chunks, chain=iter(()))

    @pl.when((t > 0) & (t < last))
    def _():
        _rwkv_chunks(*inputs, *carried, n_chunks=n_chunks, chain=chain())

    @pl.when(t == last)
    def _():
        for _ in chain():
            pass


def _rwkv_chunks(r_ref, lw_ref, k_ref, v_ref, a_ref, b_ref, g_ref, rk_ref, qe_s, yl_s, mc_s, gc_s, bonus_s, gate_s,
                 *, n_chunks, chain):
    L = CHUNK

    row = lax.broadcasted_iota(jnp.int32, (2 * L, LANES), 0)
    col = lax.broadcasted_iota(jnp.int32, (2 * L, LANES), 1)
    block_diag = (row >= L) == (col >= L)
    rt = row & (L - 1)
    ct = col & (L - 1)
    strict = ct < rt
    incl = ct <= rt
    eye = (row == col).astype(F32)
    lane = lax.broadcasted_iota(jnp.int32, (L, LANES), 1)
    first = lane < HEAD
    tr = lax.broadcasted_iota(jnp.int32, (L, L), 0)
    tc = lax.broadcasted_iota(jnp.int32, (L, L), 1)
    tri = (tc <= tr).astype(BF16)

    def stack(z):
        return jnp.concatenate([jnp.where(first, z, 0.0), jnp.where(first, 0.0, z)], axis=0)

    def split3(z):
        hi = z.astype(BF16)
        r1 = z - hi.astype(F32)
        mid = r1.astype(BF16)
        return hi, mid, (r1 - mid.astype(F32)).astype(BF16)

    rk = rk_ref[...]

    chunks = range(n_chunks)
    sls = [pl.ds(c * L, L) for c in chunks]
    R = [r_ref[0, sl, :] for sl in sls]
    LW = [lw_ref[0, sl, :] for sl in sls]
    K = [k_ref[0, sl, :] for sl in sls]
    V = [v_ref[0, sl, :] for sl in sls]
    A = [a_ref[0, sl, :] for sl in sls]
    Bv = [b_ref[0, sl, :] for sl in sls]
    def stage(fn, ticks=1):
        out = [fn(c) for c in chunks]
        for _ in range(ticks):
            next(chain, None)
        return out

    cs = stage(lambda c: _bdot(tri, jnp.concatenate(split3(LW[c]), axis=1)), ticks=2)
    cum = [z[:, :LANES] + z[:, LANES:2 * LANES] + z[:, 2 * LANES:] for z in cs]
    cum_last = [z[L - 1:L, :] for z in cum]
    inv = [jnp.exp(-z) for z in cum]
    to_end = [jnp.exp(cum_last[c] - cum[c]) for c in chunks]
    Rt = [R[c] * jnp.exp(cum[c]) for c in chunks]
    At = [A[c] * jnp.exp(cum[c] - LW[c]) for c in chunks]
    Kt = [K[c] * inv[c] for c in chunks]
    Bt = [Bv[c] * inv[c] for c in chunks]
    AtS = [stack(z) for z in At]
    RtS = [stack(z) for z in Rt]
    BhT = [stack(Bv[c] * to_end[c]).T.astype(BF16) for c in chunks]
    KhT = [stack(K[c] * to_end[c]).T.astype(BF16) for c in chunks]
    kv = stage(lambda c: _bdot(KhT[c], stack(V[c]).astype(BF16)), ticks=2)

    p0 = stage(lambda c: _dot_nt(jnp.concatenate([AtS[c][:L], RtS[c][:L]], axis=0).astype(BF16),
                                 jnp.concatenate([Bt[c], Kt[c]], axis=0).astype(BF16)), ticks=2)
    p1 = stage(lambda c: _dot_nt(jnp.concatenate([AtS[c][L:], RtS[c][L:]], axis=0).astype(BF16),
                                 jnp.concatenate([Kt[c], Bt[c]], axis=0).astype(BF16)), ticks=2)
    top = [jnp.concatenate([p0[c][:L], p1[c][:L]], axis=0) for c in chunks]
    bot = [jnp.concatenate([p0[c][L:], p1[c][L:]], axis=0) for c in chunks]
    n_ab = [jnp.where(block_diag & strict, z, 0.0) for z in top]
    m_ak = [jnp.where(block_diag | ~strict, 0.0, z) for z in top]
    m_rb = [jnp.where(block_diag & incl, z, 0.0) for z in bot]
    m_rk = [jnp.where(block_diag | ~incl, 0.0, z) for z in bot]

    tinv = [eye + z for z in n_ab]
    xb = [z.astype(BF16) for z in n_ab]
    xb = [z.astype(BF16) for z in stage(lambda c: _bdot(xb[c], xb[c]))]
    for _ in range(4):
        xt = stage(lambda c: _bdot(xb[c], jnp.concatenate([xb[c], tinv[c].astype(BF16)], axis=1)))
        xb = [z[:, :LANES].astype(BF16) for z in xt]
        tinv = [tinv[c] + xt[c][:, LANES:] for c in chunks]
    last = stage(lambda c: _bdot(xb[c], tinv[c].astype(BF16)))
    tinv = [tinv[c] + last[c] for c in chunks]

    Vb = [z.astype(BF16) for z in V]
    zy = stage(lambda c: _bdot(jnp.concatenate([m_ak[c], m_rk[c]], axis=0).astype(BF16),
                               jnp.concatenate([Vb[c], Vb[c]], axis=0)))
    z_ak = [jnp.where(block_diag, z[:2 * L], 0.0) for z in zy]
    y_rk = [jnp.where(block_diag, z[2 * L:], 0.0) for z in zy]
    wub = [z.astype(BF16) for z in stage(
        lambda c: _bdot(tinv[c].astype(BF16), jnp.concatenate([AtS[c], z_ak[c]], axis=1).astype(BF16)))]
    qy = stage(lambda c: _bdot(m_rb[c].astype(BF16), wub[c]))
    q_eff = [(RtS[c] + qy[c][:, :LANES]).astype(BF16) for c in chunks]
    y_loc = [y_rk[c] + qy[c][:, LANES:] for c in chunks]
    mg = stage(lambda c: _bdot(BhT[c], wub[c]))
    m_c = [(eye * jnp.exp(cum_last[c]) + mg[c][:, :LANES]).astype(BF16) for c in chunks]
    g_c = [mg[c][:, LANES:] + kv[c] for c in chunks]

    for _ in chain:
        pass
    for c in chunks:
        qe_s[c] = q_eff[c]
        yl_s[c] = y_loc[c]
        mc_s[c] = m_c[c]
        gc_s[c] = g_c[c]
        bonus_s[c] = _half_sums(R[c] * K[c] * rk, first) * V[c]
        gate_s[c] = g_ref[0, sls[c], :]


def _rwkv_chain(lg_ref, lb_ref, o_ref, h_ref, qe_s, yl_s, mc_s, gc_s, bonus_s, gate_s, *, n_chunks, restart):
    L = CHUNK
    lane = lax.broadcasted_iota(jnp.int32, (L, LANES), 1)
    first = lane < HEAD
    lg = lg_ref[...]
    lb = lb_ref[...]
    h = jnp.where(restart, 0.0, h_ref[...])
    ys = []
    for c in range(n_chunks):
        both = _bdot(jnp.concatenate([qe_s[c], mc_s[c]], axis=0), h.astype(BF16))
        ys.append(both[:2 * L] + yl_s[c])
        h = both[2 * L:] + gc_s[c]
        yield
    h_ref[...] = h

    for c in range(n_chunks):
        y = ys[c][:L] + ys[c][L:]
        mu = _half_sums(y, first) * (1.0 / HEAD)
        d = y - mu
        var = _half_sums(d * d, first) * (1.0 / HEAD)
        yn = d * lax.rsqrt(var + RW_LNX_EPS) * lg + lb
        o_ref[0, pl.ds(c * L, L), :] = ((yn + bonus_s[c]) * gate_s[c]).astype(BF16)


def _rwkv_time_mix(x, B, T, g, mix, w_rkv, w0, w1, w2, a0, a1, a2, g1, g2, k_k, k_a, r_k, lnx_g, lnx_b,
                   *, tm=512, tt=1024):
    M, C = x.shape
    row = pl.BlockSpec((tm, C), lambda i: (i, 0))
    prev = pl.BlockSpec((SUBLANES, C), lambda i: (jnp.maximum(i * (tm // SUBLANES) - 1, 0), 0))
    vec = lambda a: a.reshape(1, C)
    outs = pl.pallas_call(
        functools.partial(_rwkv_prep_kernel, tiles_per_seq=T // tm),
        grid=(M // tm,),
        in_specs=[row, prev, _resident((1, C)), _resident(mix.shape), _resident(w_rkv.shape),
                  _resident(w1.shape), _resident(w2.shape), _resident(a1.shape), _resident(a2.shape),
                  _resident(g1.shape), _resident(g2.shape)] + [_resident((1, C))] * 4,
        out_specs=[row] * 7,
        out_shape=[jax.ShapeDtypeStruct((M, C), F32)] * 7,
        compiler_params=_params("parallel"),
        name="rwkv_prep",
    )(x, x, vec(g), mix, w_rkv, w1, w2, a1, a2, g1, g2, vec(w0), vec(a0), vec(k_k), vec(k_a))
    seq = [z.reshape(B, T, C) for z in outs]
    nt = T // tt
    n_chunks = tt // CHUNK
    def block(t):
        return t // nt, t % nt

    slab_in = pl.BlockSpec((1, tt, LANES), lambda s, t: (*block(jnp.minimum(t, B * nt - 1)), s))
    slab_out = pl.BlockSpec((1, tt, LANES), lambda s, t: (*block(jnp.maximum(t - 1, 0)), s))
    pvec = pl.BlockSpec((1, LANES), lambda s, t: (0, s))
    stacked = (n_chunks, 2 * CHUNK, LANES)
    y = pl.pallas_call(
        functools.partial(_rwkv_scan_kernel, n_chunks=n_chunks, blocks_per_seq=nt),
        grid=(C // LANES, B * nt + 1),
        in_specs=[slab_in] * 7 + [pvec] * 3,
        out_specs=slab_out,
        out_shape=jax.ShapeDtypeStruct((B, T, C), BF16),
        scratch_shapes=[pltpu.VMEM((LANES, LANES), F32),
                        pltpu.VMEM(stacked, BF16), pltpu.VMEM(stacked, F32),
                        pltpu.VMEM(stacked, BF16), pltpu.VMEM(stacked, F32),
                        pltpu.VMEM((n_chunks, CHUNK, LANES), F32), pltpu.VMEM((n_chunks, CHUNK, LANES), F32)],
        compiler_params=_params("parallel", "arbitrary"),
        name="rwkv_scan",
    )(*seq, vec(r_k), vec(lnx_g), vec(lnx_b))
    return y.reshape(M, C)


def _pool_ffn_kernel(x_ref, xh_ref, gm_ref, pw_ref, sc_ref, g_ref, win_ref, wout_ref, o_ref,
                     *, hidden, tf, tiles_per_seq):
    ti = pl.program_id(0) % tiles_per_seq
    tm, C = x_ref.shape
    group = C // len(POOL_WINDOWS)
    gm = gm_ref[...]
    x = x_ref[...]
    h = _rms(x, gm, NORM_EPS)
    halo = jnp.where(ti == 0, 0.0, _rms(xh_ref[...], gm, NORM_EPS))
    he = jnp.concatenate([halo, h], axis=0)
    t = ti * tm + lax.broadcasted_iota(jnp.int32, (tm, 1), 0)
    ys = []
    for gi, win in enumerate(POOL_WINDOWS):
        sl = slice(gi * group, (gi + 1) * group)
        s = he[:, sl]
        shift = 1
        while shift < win:
            s = s + pltpu.roll(s, shift, 0)
            shift *= 2
        cnt = jnp.minimum(t + 1, win).astype(F32)
        delta = (s[POOL_HALO:] / cnt - h[:, sl]).astype(BF16)
        ys.append(_bdot(delta, pw_ref[gi]))
    x1 = x + jnp.concatenate(ys, axis=1) * sc_ref[...]
    _ffn_residual(x1, g_ref, win_ref, wout_ref, o_ref, hidden, tf)


def _pool_ffn(x, T, g_mix, pool_w, scale, g, w_in, w_out, *, tm=1024, tf=256):
    M, C = x.shape
    hidden = w_out.shape[0]
    blocks = tm // POOL_HALO
    row = pl.BlockSpec((tm, C), lambda i: (i, 0))
    return pl.pallas_call(
        functools.partial(_pool_ffn_kernel, hidden=hidden, tf=tf, tiles_per_seq=T // tm),
        grid=(M // tm,),
        in_specs=[row, pl.BlockSpec((POOL_HALO, C), lambda i: (jnp.maximum(i * blocks - 1, 0), 0)),
                  _resident((1, C)), _resident(pool_w.shape), _resident((1, C)), _resident((1, C)),
                  _resident((C, 2 * hidden)), _resident((hidden, C))],
        out_specs=row,
        out_shape=jax.ShapeDtypeStruct((M, C), F32),
        compiler_params=_params("parallel", vmem_limit_bytes=ATTN_VMEM_LIMIT_BYTES),
        name="pool_ffn",
    )(x, x, g_mix.reshape(1, C), pool_w, scale.reshape(1, C), g.reshape(1, C), w_in, w_out)


def kernel(x, mix_norm, ffn_norm, ffn_w_in, ffn_w_out, da_wqkv, da_wo, da_q_gain, da_k_gain, da_lambda, da_subln, rw_mix, rw_wrkv, rw_wo, rw_w0, rw_w1, rw_w2, rw_a0, rw_a1, rw_a2, rw_g1, rw_g2, rw_kk, rw_ka, rw_rk, rw_lnx_g, rw_lnx_b, pool_w, pool_scale):
    B, T, C = x.shape
    depth = mix_norm.shape[0]
    bf = lambda w: w.astype(BF16)
    xs = x.reshape(B * T, C)
    ia = ir = ip = 0
    for layer in range(depth):
        kind = layer % N_MIXERS
        ffn = (ffn_norm[layer], bf(ffn_w_in[layer]), bf(ffn_w_out[layer]))
        if kind == 0:
            lambda_init = 0.8 - 0.6 * math.exp(-0.3 * layer)
            y = _diff_attention(xs, B, T, mix_norm[layer], bf(da_wqkv[ia]), da_q_gain[ia], da_k_gain[ia],
                                da_lambda[ia], da_subln[ia], lambda_init)
            xs = _mix_ffn(xs, y, bf(da_wo[ia]), *ffn)
            ia += 1
        elif kind == 1:
            y = _rwkv_time_mix(xs, B, T, mix_norm[layer], rw_mix[ir], bf(rw_wrkv[ir]), rw_w0[ir], bf(rw_w1[ir]),
                               bf(rw_w2[ir]), rw_a0[ir], bf(rw_a1[ir]), bf(rw_a2[ir]), bf(rw_g1[ir]),
                               bf(rw_g2[ir]), rw_kk[ir], rw_ka[ir], rw_rk[ir], rw_lnx_g[ir], rw_lnx_b[ir])
            xs = _mix_ffn(xs, y, bf(rw_wo[ir]), *ffn)
            ir += 1
        else:
            xs = _pool_ffn(xs, T, mix_norm[layer], bf(pool_w[ip]), pool_scale[ip], *ffn)
            ip += 1
    return xs.reshape(B, T, C)
```
